```python
import jax, jax.numpy as jnp
from jax import lax
import numpy as np

D_MODEL = 1024
BATCH = 4
SEQ = 4096
DEPTH = 1
DEC_BATCH = 32
DEC_SEQ = 4
PAST_LEN = 8192
PAGE_SIZE = 128

LRU_WIDTH = D_MODEL // 2
LRU_BLOCKS = 8
LRU_BLOCK_DIM = LRU_WIDTH // LRU_BLOCKS
CONV_W = 4
LRU_C = 8.0
HEAD_DIM = 64
SB_HEADS = (D_MODEL // 2) // HEAD_DIM
SB_WIDTH = SB_HEADS * HEAD_DIM
MIX_WIDTH = LRU_WIDTH + SB_WIDTH
IN_COLS = 2 * LRU_WIDTH + 3 * SB_WIDTH
Q_BLOCK = 128
SB_BIAS_INIT = -8.0
N_GROUPS = 4
EXPERTS_PER_GROUP = 8
N_EXPERTS = N_GROUPS * EXPERTS_PER_GROUP
TOP_K_IN_GROUP = 2
D_EXPERT = D_MODEL // 2
EPS = 1e-6

kernel_name = "hymba_rglru_stickbreaking_hmoe_step"


def rmsnorm(x, g):
    xf = x.astype(jnp.float32)
    y = xf * lax.rsqrt(jnp.mean(xf * xf, axis=-1, keepdims=True) + EPS)
    return (y * g.astype(jnp.float32)).astype(x.dtype)


def causal_conv(u, buf, w, b):
    T = u.shape[1]
    up = jnp.concatenate([buf.astype(u.dtype), u], axis=1)
    out = b[None, None, :] + sum(w[k][None, None, :] * up[:, k:k + T] for k in range(CONV_W))
    return out, up[:, -(CONV_W - 1):]


def rg_lru(u, h0, w_a, b_a, w_i, b_i, lam):
    B, T, C = u.shape
    uf = u.astype(jnp.float32)
    ub = uf.reshape(B, T, LRU_BLOCKS, LRU_BLOCK_DIM)
    r = jax.nn.sigmoid(jnp.einsum('btnd,nde->btne', ub, w_a.astype(jnp.float32)).reshape(B, T, C) + b_a.astype(jnp.float32))
    i = jax.nn.sigmoid(jnp.einsum('btnd,nde->btne', ub, w_i.astype(jnp.float32)).reshape(B, T, C) + b_i.astype(jnp.float32))
    log_a = -LRU_C * r * jax.nn.softplus(-lam.astype(jnp.float32))
    a = jnp.exp(log_a)
    xin = jnp.sqrt(-jnp.expm1(2.0 * log_a)) * (i * uf)

    def step(h, inp):
        a_t, x_t = inp
        h = a_t * h + x_t
        return h, h

    h_T, hs = lax.scan(step, h0.astype(jnp.float32), (a.transpose(1, 0, 2), xin.transpose(1, 0, 2)))
    return hs.transpose(1, 0, 2), h_T


def sb_block(q, k, v, q_pos, k_pos, b_sb):
    z = jnp.einsum('bqhd,bshd->bhqs', q.astype(jnp.float32), k.astype(jnp.float32)) * (HEAD_DIM ** -0.5)
    z = z + b_sb.astype(jnp.float32)[None, :, None, None]
    mask = (k_pos[None, :] < q_pos[:, None])[None, None]
    log_1mb = jnp.where(mask, jax.nn.log_sigmoid(-z), 0.0)
    later = lax.cumsum(log_1mb, axis=3, reverse=True) - log_1mb
    wts = jnp.where(mask, jnp.exp(jax.nn.log_sigmoid(z) + later), 0.0)
    return jnp.einsum('bhqs,bshd->bqhd', wts, v.astype(jnp.float32))


def sb_attend(q, k, v, q_pos, k_pos, b_sb):
    B, T, H, Dh = q.shape
    qb = min(Q_BLOCK, T)
    nb = -(-T // qb)
    pad = nb * qb - T
    q = jnp.pad(q, ((0, 0), (0, pad), (0, 0), (0, 0)))
    q_pos = jnp.pad(q_pos, (0, pad))
    q_blocks = q.reshape(B, nb, qb, H, Dh).transpose(1, 0, 2, 3, 4)
    pos_blocks = q_pos.reshape(nb, qb)
    out = lax.map(lambda a: sb_block(a[0], k, v, a[1], k_pos, b_sb), (q_blocks, pos_blocks))
    return out.transpose(1, 0, 2, 3, 4).reshape(B, nb * qb, H, Dh)[:, :T]


def hier_moe(h, w_rg, b_rg, w_re, b_re, w_eg, w_eu, w_ed):
    B, T, D = h.shape
    t = h.reshape(B * T, D)
    g_logits = (t @ w_rg + b_rg).astype(jnp.float32)
    g_idx = jnp.argmax(g_logits, axis=-1)
    p_g = jnp.max(jax.nn.softmax(g_logits, axis=-1), axis=-1, keepdims=True)
    e_logits = (t @ w_re + b_re).astype(jnp.float32).reshape(-1, N_GROUPS, EXPERTS_PER_GROUP)
    in_group = jnp.einsum('ng,nge->ne', jax.nn.one_hot(g_idx, N_GROUPS, dtype=jnp.float32), e_logits)
    top_v, top_i = lax.top_k(in_group, TOP_K_IN_GROUP)
    w_sel = jax.nn.softmax(top_v, axis=-1) * p_g
    e_sel = g_idx[:, None] * EXPERTS_PER_GROUP + top_i
    combine = jnp.einsum('nk,nke->ne', w_sel, jax.nn.one_hot(e_sel, N_EXPERTS, dtype=jnp.float32))
    y = jnp.zeros((B * T, D), jnp.float32)
    for e in range(N_EXPERTS):
        a = jax.nn.silu(t @ w_eg[e]) * (t @ w_eu[e])
        y = y + combine[:, e:e + 1] * (a @ w_ed[e]).astype(jnp.float32)
    return y.reshape(B, T, D).astype(h.dtype)


def layer(x, lru_h0, conv_buf, k_past, v_past, pos0, p):
    B, T, _ = x.shape
    hn = rmsnorm(x, p['g_mix'])
    proj = jnp.einsum('btd,dc->btc', hn, p['w_in'])
    xl = proj[..., :LRU_WIDTH]
    gl = proj[..., LRU_WIDTH:2 * LRU_WIDTH]
    o = 2 * LRU_WIDTH
    q = proj[..., o:o + SB_WIDTH].reshape(B, T, SB_HEADS, HEAD_DIM)
    k = proj[..., o + SB_WIDTH:o + 2 * SB_WIDTH].reshape(B, T, SB_HEADS, HEAD_DIM)
    v = proj[..., o + 2 * SB_WIDTH:o + 3 * SB_WIDTH].reshape(B, T, SB_HEADS, HEAD_DIM)
    u, new_buf = causal_conv(xl, conv_buf, p['conv_w'], p['conv_b'])
    hs, h_T = rg_lru(u, lru_h0, p['w_a'], p['b_a'], p['w_i'], p['b_i'], p['lam'])
    y_lru = hs * jax.nn.gelu(gl.astype(jnp.float32))
    k_all = jnp.concatenate([k_past.astype(k.dtype), k], axis=1)
    v_all = jnp.concatenate([v_past.astype(v.dtype), v], axis=1)
    k_pos = jnp.arange(pos0 + T, dtype=jnp.int32)
    q_pos = pos0 + jnp.arange(T, dtype=jnp.int32)
    y_sb = sb_attend(q, k_all, v_all, q_pos, k_pos, p['b_sb']).reshape(B, T, SB_WIDTH)
    mix = jnp.concatenate([rmsnorm(y_lru, p['g_out_lru']), rmsnorm(y_sb, p['g_out_sb'])], axis=-1).astype(x.dtype)
    x = x + jnp.einsum('btc,cd->btd', mix, p['w_out'])
    x = x + hier_moe(rmsnorm(x, p['g_ffn']), p['w_rg'], p['b_rg'], p['w_re'], p['b_re'], p['w_eg'], p['w_eu'], p['w_ed'])
    return x, k, v, h_T, new_buf


def setup_inputs(seed: int = 0) -> dict:
    key = jax.random.key(seed)
    ks = jax.random.split(key, 32)
    f32 = jnp.float32
    n_pages = PAST_LEN // PAGE_SIZE
    n_pool = (DEC_BATCH * n_pages * 5) // 4
    nrm = lambda k, s, sc: jax.random.normal(k, s, f32) * sc
    perm = jax.random.permutation(ks[0], n_pool)[:DEC_BATCH * n_pages]
    u = jax.random.uniform(ks[1], (DEPTH, LRU_WIDTH), f32, 0.9, 0.999)
    a0 = u ** (1.0 / LRU_C)
    lam = jnp.log(a0) - jnp.log1p(-a0)
    return {
        "x_prompt": nrm(ks[2], (BATCH, SEQ, D_MODEL), 1.0),
        "x_sample": nrm(ks[3], (DEC_BATCH, DEC_SEQ, D_MODEL), 1.0),
        "cache_k": nrm(ks[4], (DEPTH, n_pool, PAGE_SIZE, SB_HEADS, HEAD_DIM), 1.0),
        "cache_v": nrm(ks[5], (DEPTH, n_pool, PAGE_SIZE, SB_HEADS, HEAD_DIM), 1.0),
        "state_lru_h": nrm(ks[6], (DEPTH, DEC_BATCH, LRU_WIDTH), 0.5),
        "state_conv": nrm(ks[7], (DEPTH, DEC_BATCH, CONV_W - 1, LRU_WIDTH), 1.0),
        "page_table": perm.reshape(DEC_BATCH, n_pages).astype(jnp.int32),
        "g_mix": 1.0 + nrm(ks[8], (DEPTH, D_MODEL), 0.02),
        "w_in": nrm(ks[9], (DEPTH, D_MODEL, IN_COLS), D_MODEL ** -0.5),
        "conv_w": nrm(ks[10], (DEPTH, CONV_W, LRU_WIDTH), CONV_W ** -0.5),
        "conv_b": nrm(ks[11], (DEPTH, LRU_WIDTH), 0.01),
        "w_a": nrm(ks[12], (DEPTH, LRU_BLOCKS, LRU_BLOCK_DIM, LRU_BLOCK_DIM), LRU_BLOCK_DIM ** -0.5),
        "b_a": nrm(ks[13], (DEPTH, LRU_WIDTH), 0.01),
        "w_i": nrm(ks[14], (DEPTH, LRU_BLOCKS, LRU_BLOCK_DIM, LRU_BLOCK_DIM), LRU_BLOCK_DIM ** -0.5),
        "b_i": nrm(ks[15], (DEPTH, LRU_WIDTH), 0.01),
        "lam": lam,
        "b_sb": SB_BIAS_INIT + nrm(ks[28], (DEPTH, SB_HEADS), 0.5),
        "g_out_lru": 1.0 + nrm(ks[16], (DEPTH, LRU_WIDTH), 0.02),
        "g_out_sb": 1.0 + nrm(ks[17], (DEPTH, SB_WIDTH), 0.02),
        "w_out": nrm(ks[18], (DEPTH, MIX_WIDTH, D_MODEL), MIX_WIDTH ** -0.5),
        "g_ffn": 1.0 + nrm(ks[19], (DEPTH, D_MODEL), 0.02),
        "w_rg": nrm(ks[20], (DEPTH, D_MODEL, N_GROUPS), D_MODEL ** -0.5),
        "b_rg": nrm(ks[21], (DEPTH, N_GROUPS), 0.01),
        "w_re": nrm(ks[22], (DEPTH, D_MODEL, N_EXPERTS), D_MODEL ** -0.5),
        "b_re": nrm(ks[23], (DEPTH, N_EXPERTS), 0.01),
        "w_eg": nrm(ks[24], (DEPTH, N_EXPERTS, D_MODEL, D_EXPERT), D_MODEL ** -0.5),
        "w_eu": nrm(ks[25], (DEPTH, N_EXPERTS, D_MODEL, D_EXPERT), D_MODEL ** -0.5),
        "w_ed": nrm(ks[26], (DEPTH, N_EXPERTS, D_EXPERT, D_MODEL), D_EXPERT ** -0.5),
        "g_final": 1.0 + nrm(ks[27], (D_MODEL,), 0.02),
    }


def reference(x_prompt, x_sample, cache_k, cache_v, state_lru_h, state_conv, page_table,
              g_mix, w_in, conv_w, conv_b, w_a, b_a, w_i, b_i, lam, b_sb, g_out_lru, g_out_sb, w_out,
              g_ffn, w_rg, b_rg, w_re, b_re, w_eg, w_eu, w_ed, g_final):
    n_seq, n_pages = page_table.shape
    past_len = n_pages * cache_k.shape[2]
    Bp = x_prompt.shape[0]
    xp, xs = x_prompt, x_sample
    kp_l, vp_l, hp_l, cp_l, ks_l, vs_l, hs_l, cs_l = [], [], [], [], [], [], [], []
    for l in range(DEPTH):
        p = {'g_mix': g_mix[l], 'w_in': w_in[l], 'conv_w': conv_w[l], 'conv_b': conv_b[l],
             'w_a': w_a[l], 'b_a': b_a[l], 'w_i': w_i[l], 'b_i': b_i[l], 'lam': lam[l], 'b_sb': b_sb[l],
             'g_out_lru': g_out_lru[l], 'g_out_sb': g_out_sb[l], 'w_out': w_out[l],
             'g_ffn': g_ffn[l], 'w_rg': w_rg[l], 'b_rg': b_rg[l], 'w_re': w_re[l], 'b_re': b_re[l],
             'w_eg': w_eg[l], 'w_eu': w_eu[l], 'w_ed': w_ed[l]}
        empty = jnp.zeros((Bp, 0, SB_HEADS, HEAD_DIM), xp.dtype)
        xp, kp, vp, hp, cp = layer(xp, jnp.zeros((Bp, LRU_WIDTH), jnp.float32),
                                   jnp.zeros((Bp, CONV_W - 1, LRU_WIDTH), xp.dtype),
                                   empty, empty, 0, p)
        k_past = cache_k[l][page_table].reshape(n_seq, past_len, SB_HEADS, HEAD_DIM)
        v_past = cache_v[l][page_table].reshape(n_seq, past_len, SB_HEADS, HEAD_DIM)
        xs, ksn, vsn, hsn, csn = layer(xs, state_lru_h[l], state_conv[l], k_past, v_past, past_len, p)
        kp_l.append(kp); vp_l.append(vp); hp_l.append(hp); cp_l.append(cp)
        ks_l.append(ksn); vs_l.append(vsn); hs_l.append(hsn); cs_l.append(csn)
    y_prompt = rmsnorm(xp, g_final)
    y_sample = rmsnorm(xs, g_final)
    return (y_prompt, y_sample,
            jnp.stack(kp_l), jnp.stack(vp_l), jnp.stack(hp_l), jnp.stack(cp_l),
            jnp.stack(ks_l), jnp.stack(vs_l), jnp.stack(hs_l), jnp.stack(cs_l))
```

```python
import functools
import math

import jax
import jax.numpy as jnp
from jax import lax
from jax.experimental import pallas as pl
from jax.experimental.pallas import tpu as pltpu

F32 = jnp.float32
BF16 = jnp.bfloat16
I32 = jnp.int32

D_MODEL = 1024
LRU_W = 512
SB_W = 512
HEADS = 8
HEAD_DIM = 64
CONV_W = 4
LRU_C = 8.0
EPS = 1e-6
N_GROUPS = 4
EXPERTS_PER_GROUP = 8
N_EXPERTS = 32
D_EXPERT = 512

LANES = 128
SUBLANES = 8
ROW_CHUNKS = D_MODEL // LANES

TM_PROJ = 512
TQ = 256
TM_OUT = 512
TM_EXP = 256
TD = 256
PAGES_PER_STEP = 8
ROUTER_COLS = LANES
VMEM_LIMIT = 56 * 1024 * 1024


def _softplus(z):
    return jnp.maximum(z, 0.0) + jnp.log1p(jnp.exp(-jnp.abs(z)))


def _rms(x, g):
    return x * lax.rsqrt(jnp.mean(x * x, axis=-1, keepdims=True) + EPS) * g


def _gelu_tanh(x):
    c = math.sqrt(2.0 / math.pi)
    return x * (0.5 * (1.0 + jnp.tanh(c * (x + 0.044715 * (x * x * x)))))


def _neg_expm1(x):
    u = jnp.exp(x)
    near = x > -1.0
    ratio = x / jnp.where(near & (u < 1.0), jnp.log(u), -1.0)
    return jnp.where(near, jnp.where(u < 1.0, (1.0 - u) * ratio, -x), 1.0 - u)


def _lru_gates(u, gates, lam):
    r = jax.nn.sigmoid(gates[:, :LRU_W])
    i = jax.nn.sigmoid(gates[:, LRU_W:])
    log_a = (-LRU_C) * r * _softplus(-lam)
    a = jnp.exp(log_a)
    xin = jnp.sqrt(_neg_expm1(2.0 * log_a)) * (i * u)
    return a, xin


def _split_bf16(x):
    hi = x.astype(BF16)
    lo = (x - hi.astype(F32)).astype(BF16)
    return hi, lo


def _inproj_lru_kernel(x_ref, gmix_ref, win_ref, cw_ref, cb_ref, wg_ref, bg_ref, lam_ref, glru_ref,
                       q_ref, k_ref, v_ref, kb_ref, vt_ref, ylru_ref, ht_ref, cbuf_ref,
                       xlp_ref, a_ref, s_ref, h_ref, *, tm, tk):
    t = pl.program_id(1)
    nt = pl.num_programs(1)

    @pl.when(t == 0)
    def _init():
        xlp_ref[0:SUBLANES, :] = jnp.zeros((SUBLANES, LRU_W), F32)
        h_ref[...] = jnp.zeros_like(h_ref)

    hn = _rms(x_ref[...], gmix_ref[...]).astype(BF16)

    def proj(c):
        return jnp.dot(hn, win_ref[:, c * LRU_W:(c + 1) * LRU_W], preferred_element_type=F32)

    xl = proj(0)
    gl = proj(1)
    q_ref[...] = proj(2).astype(BF16)
    k = proj(3)
    k_ref[...] = k
    kb_ref[...] = k.astype(BF16)
    v = proj(4)
    v_ref[...] = v
    for c in range(tm // tk):
        vt_ref[0, c] = v[c * tk:(c + 1) * tk, :].T.astype(BF16)

    xlp_ref[SUBLANES:SUBLANES + tm, :] = xl
    cw = cw_ref[...]
    u = cb_ref[...] + cw[3:4, :] * xl
    for j in range(1, CONV_W):
        u = u + cw[3 - j:4 - j, :] * xlp_ref[pl.ds(SUBLANES - j, tm), :]
    xlp_ref[0:SUBLANES, :] = xl[tm - SUBLANES:tm, :]

    gates = jnp.dot(u.astype(BF16), wg_ref[...], preferred_element_type=F32) + bg_ref[...]
    a, xin = _lru_gates(u, gates, lam_ref[...])
    a_ref[...] = a
    s_ref[...] = xin

    row = lax.broadcasted_iota(I32, (SUBLANES, LRU_W), 0)

    def group(g, h):
        r0 = pl.multiple_of(g * SUBLANES, SUBLANES)
        av = a_ref[pl.ds(r0, SUBLANES), :]
        xv = s_ref[pl.ds(r0, SUBLANES), :]
        for d in (1, 2, 4):
            valid = row >= d
            xv = jnp.where(valid, xv + av * pltpu.roll(xv, d, 0), xv)
            av = jnp.where(valid, av * pltpu.roll(av, d, 0), av)
        hv = xv + av * h
        s_ref[pl.ds(r0, SUBLANES), :] = hv
        return hv[SUBLANES - 1:SUBLANES, :]

    h = lax.fori_loop(0, tm // SUBLANES, group, h_ref[0:1, :])
    h_ref[0:1, :] = h

    y = s_ref[...] * _gelu_tanh(gl)
    ylru_ref[...] = _rms(y, glru_ref[...]).astype(BF16)

    @pl.when(t == nt - 1)
    def _fin():
        ht_ref[0] = h
        cbuf_ref[0] = xl[tm - (CONV_W - 1):tm, :]


def _inproj_lru_prompt(x2, g_mix, w_in_bf, conv_w, conv_b, w_gates, b_gates, lam, g_lru, batch, seq):
    tm, tk = TM_PROJ, TQ
    nt = seq // tm
    n = batch * seq
    row_spec = lambda w: pl.BlockSpec((tm, w), lambda b, t: (b * nt + t, 0))
    full = lambda a: pl.BlockSpec(a.shape, lambda b, t: (0,) * a.ndim)
    out_shape = (
        jax.ShapeDtypeStruct((n, SB_W), BF16),
        jax.ShapeDtypeStruct((n, SB_W), F32),
        jax.ShapeDtypeStruct((n, SB_W), F32),
        jax.ShapeDtypeStruct((n, SB_W), BF16),
        jax.ShapeDtypeStruct((batch, seq // tk, SB_W, tk), BF16),
        jax.ShapeDtypeStruct((n, LRU_W), BF16),
        jax.ShapeDtypeStruct((batch, 1, LRU_W), F32),
        jax.ShapeDtypeStruct((batch, CONV_W - 1, LRU_W), F32),
    )
    out_specs = (
        row_spec(SB_W), row_spec(SB_W), row_spec(SB_W), row_spec(SB_W),
        pl.BlockSpec((1, tm // tk, SB_W, tk), lambda b, t: (b, t, 0, 0)),
        row_spec(LRU_W),
        pl.BlockSpec((1, 1, LRU_W), lambda b, t: (b, 0, 0)),
        pl.BlockSpec((1, CONV_W - 1, LRU_W), lambda b, t: (b, 0, 0)),
    )
    args = (x2, g_mix, w_in_bf, conv_w, conv_b, w_gates, b_gates, lam, g_lru)
    return pl.pallas_call(
        functools.partial(_inproj_lru_kernel, tm=tm, tk=tk),
        grid=(batch, nt),
        in_specs=[row_spec(D_MODEL)] + [full(a) for a in args[1:]],
        out_specs=out_specs,
        out_shape=out_shape,
        scratch_shapes=[
            pltpu.VMEM((tm + SUBLANES, LRU_W), F32),
            pltpu.VMEM((tm, LRU_W), F32),
            pltpu.VMEM((tm, LRU_W), F32),
            pltpu.VMEM((SUBLANES, LRU_W), F32),
        ],
        compiler_params=pltpu.CompilerParams(
            dimension_semantics=("arbitrary", "arbitrary"), vmem_limit_bytes=VMEM_LIMIT),
        name="inproj_lru_prompt",
    )(*args)


def _inproj_lru_sample_kernel(x_ref, h0_ref, buf_ref, gmix_ref, win_ref, cw_ref, cb_ref, wg_ref, bg_ref,
                              lam_ref, glru_ref, q_ref, k_ref, v_ref, ylru_ref, ht_ref, cbuf_ref,
                              *, nseq, nt):
    hn = _rms(x_ref[...], gmix_ref[...]).astype(BF16)

    def proj(c):
        return jnp.dot(hn, win_ref[:, c * LRU_W:(c + 1) * LRU_W], preferred_element_type=F32)

    xl = proj(0)
    gl = proj(1)
    q_ref[...] = proj(2)
    k_ref[...] = proj(3)
    v_ref[...] = proj(4)

    hist = [buf_ref[m * nseq:(m + 1) * nseq, :] for m in range(CONV_W - 1)]
    hist += [xl[t * nseq:(t + 1) * nseq, :] for t in range(nt)]
    cw = cw_ref[...]
    us = []
    for t in range(nt):
        ut = cb_ref[...] + cw[0:1, :] * hist[t]
        for j in range(1, CONV_W):
            ut = ut + cw[j:j + 1, :] * hist[t + j]
        us.append(ut)
    u = jnp.concatenate(us, axis=0)
    gates = jnp.dot(u.astype(BF16), wg_ref[...], preferred_element_type=F32) + bg_ref[...]
    a, xin = _lru_gates(u, gates, lam_ref[...])
    h = h0_ref[...]
    hs = []
    for t in range(nt):
        h = a[t * nseq:(t + 1) * nseq, :] * h + xin[t * nseq:(t + 1) * nseq, :]
        hs.append(h)
    y = jnp.concatenate(hs, axis=0) * _gelu_tanh(gl)
    ylru_ref[...] = _rms(y, glru_ref[...]).astype(BF16)
    ht_ref[...] = h
    cbuf_ref[...] = jnp.concatenate(hist[nt:nt + CONV_W - 1], axis=0)


def _inproj_lru_sample(x_tm, h0, buf_tm, g_mix, w_in_bf, conv_w, conv_b, w_gates, b_gates, lam, g_lru,
                       nseq, nt):
    n = nseq * nt
    out_shape = (
        jax.ShapeDtypeStruct((n, SB_W), F32),
        jax.ShapeDtypeStruct((n, SB_W), F32),
        jax.ShapeDtypeStruct((n, SB_W), F32),
        jax.ShapeDtypeStruct((n, LRU_W), BF16),
        jax.ShapeDtypeStruct((nseq, LRU_W), F32),
        jax.ShapeDtypeStruct(((CONV_W - 1) * nseq, LRU_W), F32),
    )
    return pl.pallas_call(
        functools.partial(_inproj_lru_sample_kernel, nseq=nseq, nt=nt),
        out_shape=out_shape,
        compiler_params=pltpu.CompilerParams(vmem_limit_bytes=VMEM_LIMIT),
        name="inproj_lru_sample",
    )(x_tm, h0, buf_tm, g_mix, w_in_bf, conv_w, conv_b, w_gates, b_gates, lam, g_lru)


def _sb_prompt_kernel(bsb_ref, q_ref, k_ref, vt_ref, g_ref, o_ref, acc_ref, *, tq):
    i = pl.program_id(1)
    tk = tq
    rowk = lax.broadcasted_iota(I32, (tk, tq), 0)
    colq = lax.broadcasted_iota(I32, (tk, tq), 1)
    diag_mask = rowk < colq
    later_mat = (lax.broadcasted_iota(I32, (tk, tk), 1) > lax.broadcasted_iota(I32, (tk, tk), 0)).astype(BF16)
    lane = lax.broadcasted_iota(I32, (tq, LANES), 1)
    scale = jnp.asarray(HEAD_DIM ** -0.5, BF16)

    for h in range(HEADS):
        pair, half = divmod(h, LANES // HEAD_DIM)
        cols = slice(pair * LANES, (pair + 1) * LANES)
        in_head = (lane >= half * HEAD_DIM) & (lane < (half + 1) * HEAD_DIM)
        qm = jnp.where(in_head, q_ref[:, cols] * scale, jnp.zeros((), BF16))
        bh = bsb_ref[h]

        def block(j, carry, masked, qm=qm, bh=bh, cols=cols, h=h):
            c, acc = carry
            k0 = pl.multiple_of(j * tk, tk)
            kb = k_ref[pl.ds(k0, tk), cols]
            s = lax.dot_general(kb, qm, (((1,), (1,)), ((), ())), preferred_element_type=F32) + bh
            l1 = -_softplus(s)
            if masked:
                l1 = jnp.where(diag_mask, l1, 0.0)
            hi, lo = _split_bf16(l1)
            ex = (jnp.dot(later_mat, hi, preferred_element_type=F32)
                  + jnp.dot(later_mat, lo, preferred_element_type=F32))
            w = jnp.exp(s + l1 + ex + c)
            if masked:
                w = jnp.where(diag_mask, w, 0.0)
            vt = vt_ref[0, j, h * HEAD_DIM:(h + 1) * HEAD_DIM, :]
            acc = acc + jnp.dot(vt, w.astype(BF16), preferred_element_type=F32)
            c = c + ex[0:1, :] + l1[0:1, :]
            return c, acc

        carry = (jnp.zeros((1, tq), F32), jnp.zeros((HEAD_DIM, tq), F32))
        carry = block(i, carry, True)
        carry = lax.fori_loop(0, i, lambda jj, cr, block=block: block(i - 1 - jj, cr, False), carry)
        acc_ref[h * HEAD_DIM:(h + 1) * HEAD_DIM, :] = carry[1]

    o_ref[...] = _rms(acc_ref[...].T, g_ref[...]).astype(BF16)


def _sb_prompt(b_sb, q_bf, k_bf, vt_bf, g_sb, batch, seq):
    tq = TQ
    nq = seq // tq
    n = batch * seq
    return pl.pallas_call(
        functools.partial(_sb_prompt_kernel, tq=tq),
        grid=(batch, nq),
        in_specs=[
            pl.BlockSpec(memory_space=pltpu.SMEM),
            pl.BlockSpec((tq, SB_W), lambda b, i: (b * nq + i, 0)),
            pl.BlockSpec((seq, SB_W), lambda b, i: (b, 0)),
            pl.BlockSpec((1, nq, SB_W, tq), lambda b, i: (b, 0, 0, 0)),
            pl.BlockSpec((1, SB_W), lambda b, i: (0, 0)),
        ],
        out_specs=pl.BlockSpec((tq, SB_W), lambda b, i: (b * nq + i, 0)),
        out_shape=jax.ShapeDtypeStruct((n, SB_W), BF16),
        scratch_shapes=[pltpu.VMEM((SB_W, tq), F32)],
        compiler_params=pltpu.CompilerParams(
            dimension_semantics=("arbitrary", "arbitrary"), vmem_limit_bytes=VMEM_LIMIT),
        name="sb_prompt",
    )(b_sb, q_bf, k_bf, vt_bf, g_sb)


def _sb_sample_kernel(pt_ref, bsb_ref, q_ref, kown_ref, vown_ref, g_ref, *rest, npg, nt, page):
    kp = rest[:npg]
    vp = rest[npg:2 * npg]
    o_ref = rest[2 * npg]
    qbd_ref, acc_ref, c_ref = rest[2 * npg + 1:]
    ci = pl.program_id(1)
    nc = pl.num_programs(1)
    rows = nt * HEADS
    rowi = lax.broadcasted_iota(I32, (rows, 1), 0)
    row_h = rowi & (HEADS - 1)
    row_t = rowi >> int(math.log2(HEADS))
    bias = jnp.zeros((rows, 1), F32)
    for h in range(HEADS):
        bias = jnp.where(row_h == h, bsb_ref[h], bias)
    head_of_lane = lax.broadcasted_iota(I32, (HEADS, SB_W), 1) // HEAD_DIM
    head_mask = head_of_lane == lax.broadcasted_iota(I32, (HEADS, SB_W), 0)

    @pl.when(ci == 0)
    def _init():
        q = q_ref[0] * (HEAD_DIM ** -0.5)
        qbd = jnp.concatenate(
            [jnp.where(head_mask, jnp.broadcast_to(q[t:t + 1, :], (HEADS, SB_W)), 0.0) for t in range(nt)],
            axis=0)
        qbd_ref[...] = qbd.astype(BF16)
        kown = kown_ref[0]
        vown = vown_ref[0]
        c = jnp.zeros((rows, 1), F32)
        acc = jnp.zeros((rows, SB_W), F32)
        for s in range(nt - 1, -1, -1):
            z = jnp.sum(qbd * kown[s:s + 1, :], axis=-1, keepdims=True) + bias
            vis = s < row_t
            l1 = jnp.where(vis, -_softplus(z), 0.0)
            w = jnp.where(vis, jnp.exp(z + l1 + c), 0.0)
            acc = acc + w * vown[s:s + 1, :]
            c = c + l1
        acc_ref[...] = acc
        c_ref[...] = jnp.broadcast_to(c, (rows, LANES))

    later_mat = (lax.broadcasted_iota(I32, (page, page), 0) > lax.broadcasted_iota(I32, (page, page), 1)).astype(BF16)
    qbd = qbd_ref[...]
    c = c_ref[...]
    acc = acc_ref[...]
    for i in range(npg):
        kb = kp[i][0].astype(BF16)
        vb = vp[i][0].astype(BF16)
        s = lax.dot_general(qbd, kb, (((1,), (1,)), ((), ())), preferred_element_type=F32) + bias
        l1 = -_softplus(s)
        hi, lo = _split_bf16(l1)
        ex = (jnp.dot(hi, later_mat, preferred_element_type=F32)
              + jnp.dot(lo, later_mat, preferred_element_type=F32))
        w = jnp.exp(s + l1 + ex + c)
        acc = acc + jnp.dot(w.astype(BF16), vb, preferred_element_type=F32)
        c = c + jnp.broadcast_to(ex[:, 0:1] + l1[:, 0:1], (rows, LANES))
    acc_ref[...] = acc
    c_ref[...] = c

    @pl.when(ci == nc - 1)
    def _fin():
        ys = [jnp.sum(jnp.where(head_mask, acc[t * HEADS:(t + 1) * HEADS, :], 0.0), axis=0, keepdims=True)
              for t in range(nt)]
        o_ref[0] = _rms(jnp.concatenate(ys, axis=0), g_ref[...])


def _sb_sample(page_table, b_sb, q_s, k_own, v_own, g_sb, cache_k, cache_v):
    nseq, npages = page_table.shape
    nt = q_s.shape[1]
    npool, page = cache_k.shape[0], cache_k.shape[1]
    npg = PAGES_PER_STEP
    nc = npages // npg
    rows = nt * HEADS

    def page_spec(i):
        return pl.BlockSpec(
            (1, page, SB_W),
            lambda s, c, pt, i=i: (pt[s * npages + (npages - 1) - (c * npg + i)], 0, 0))

    seq_spec = pl.BlockSpec((1, nt, SB_W), lambda s, c, pt: (s, 0, 0))
    grid_spec = pltpu.PrefetchScalarGridSpec(
        num_scalar_prefetch=1,
        grid=(nseq, nc),
        in_specs=[pl.BlockSpec(memory_space=pltpu.SMEM), seq_spec, seq_spec, seq_spec,
                  pl.BlockSpec((1, SB_W), lambda s, c, pt: (0, 0))]
                 + [page_spec(i) for i in range(npg)] + [page_spec(i) for i in range(npg)],
        out_specs=pl.BlockSpec((1, nt, SB_W), lambda s, c, pt: (s, 0, 0)),
        scratch_shapes=[pltpu.VMEM((rows, SB_W), BF16), pltpu.VMEM((rows, SB_W), F32),
                        pltpu.VMEM((rows, LANES), F32)],
    )
    return pl.pallas_call(
        functools.partial(_sb_sample_kernel, npg=npg, nt=nt, page=page),
        grid_spec=grid_spec,
        out_shape=jax.ShapeDtypeStruct((nseq, nt, SB_W), F32),
        compiler_params=pltpu.CompilerParams(
            dimension_semantics=("arbitrary", "arbitrary"), vmem_limit_bytes=VMEM_LIMIT),
        name="sb_sample",
    )(page_table.reshape(-1), b_sb, q_s, k_own, v_own, g_sb, *([cache_k] * npg), *([cache_v] * npg))


def _outproj_router_kernel(x_ref, yl_ref, ys_ref, wout_ref, gffn_ref, wrh_ref, wrl_ref, br_ref,
                           x1_ref, h8_ref, e_ref, w_ref, *, tm):
    x1 = (x_ref[...]
          + jnp.dot(yl_ref[...], wout_ref[0:LRU_W, :], preferred_element_type=F32)
          + jnp.dot(ys_ref[...], wout_ref[LRU_W:LRU_W + SB_W, :], preferred_element_type=F32))
    x1_ref[...] = x1
    h = _rms(x1, gffn_ref[...])
    for j in range(ROW_CHUNKS):
        h8_ref[pl.ds(j, tm, stride=ROW_CHUNKS), :] = h[:, j * LANES:(j + 1) * LANES]

    hi, lo = _split_bf16(h)
    wrh = wrh_ref[...]
    logits = (jnp.dot(hi, wrh, preferred_element_type=F32) + jnp.dot(lo, wrh, preferred_element_type=F32)
              + jnp.dot(hi, wrl_ref[...], preferred_element_type=F32) + br_ref[...])

    lane = lax.broadcasted_iota(I32, (tm, ROUTER_COLS), 1)
    big = jnp.asarray(ROUTER_COLS, I32)
    neg = jnp.asarray(-jnp.inf, F32)
    gl = jnp.where(lane < N_GROUPS, logits, neg)
    gmax = jnp.max(gl, axis=-1, keepdims=True)
    g_idx = jnp.min(jnp.where(gl == gmax, lane, big), axis=-1, keepdims=True)
    p_g = 1.0 / jnp.sum(jnp.exp(gl - gmax), axis=-1, keepdims=True)
    e_lo = N_GROUPS + g_idx * EXPERTS_PER_GROUP
    el = jnp.where((lane >= e_lo) & (lane < e_lo + EXPERTS_PER_GROUP), logits, neg)
    m1 = jnp.max(el, axis=-1, keepdims=True)
    i1 = jnp.min(jnp.where(el == m1, lane, big), axis=-1, keepdims=True)
    el2 = jnp.where(lane == i1, neg, el)
    m2 = jnp.max(el2, axis=-1, keepdims=True)
    i2 = jnp.min(jnp.where(el2 == m2, lane, big), axis=-1, keepdims=True)
    r = jnp.exp(m2 - m1)
    w1 = p_g / (1.0 + r)
    w2 = w1 * r
    e_ref[...] = jnp.where(lane == 0, i1 - N_GROUPS, jnp.where(lane == 1, i2 - N_GROUPS, 0))
    w_ref[...] = jnp.where(lane == 0, w1, jnp.where(lane == 1, w2, 0.0))


def _outproj_router(x2, ylru, ysb, w_out_bf, g_ffn, wr_hi, wr_lo, b_r, tm):
    n = x2.shape[0]
    row_spec = lambda w: pl.BlockSpec((tm, w), lambda t: (t, 0))
    full = lambda a: pl.BlockSpec(a.shape, lambda t: (0,) * a.ndim)
    return pl.pallas_call(
        functools.partial(_outproj_router_kernel, tm=tm),
        grid=(n // tm,),
        in_specs=[row_spec(D_MODEL), row_spec(LRU_W), row_spec(SB_W), full(w_out_bf), full(g_ffn),
                  full(wr_hi), full(wr_lo), full(b_r)],
        out_specs=(row_spec(D_MODEL), pl.BlockSpec((tm * ROW_CHUNKS, LANES), lambda t: (t, 0)),
                   row_spec(ROUTER_COLS), row_spec(ROUTER_COLS)),
        out_shape=(jax.ShapeDtypeStruct((n, D_MODEL), F32),
                   jax.ShapeDtypeStruct((n * ROW_CHUNKS, LANES), F32),
                   jax.ShapeDtypeStruct((n, ROUTER_COLS), I32),
                   jax.ShapeDtypeStruct((n, ROUTER_COLS), F32)),
        compiler_params=pltpu.CompilerParams(
            dimension_semantics=("arbitrary",), vmem_limit_bytes=VMEM_LIMIT),
        name="outproj_router",
    )(x2, ylru, ysb, w_out_bf, g_ffn, wr_hi, wr_lo, b_r)


def _row_tile(ref, row):
    return ref.at[pl.ds(pl.multiple_of(row * ROW_CHUNKS, ROW_CHUNKS), ROW_CHUNKS)]


def _dispatch_kernel(pos_ref, h8_hbm, xs_in_hbm, xs_hbm, sem, *, td):
    del xs_in_hbm
    base = pl.program_id(0) * td

    def copy(r, k):
        return pltpu.make_async_copy(_row_tile(h8_hbm, base + r), _row_tile(xs_hbm, pos_ref[0, 0, 2 * r + k]), sem)

    def start(r, c):
        copy(r, 0).start()
        copy(r, 1).start()
        return c

    def wait(r, c):
        copy(r, 0).wait()
        copy(r, 1).wait()
        return c

    lax.fori_loop(0, td, start, 0)
    lax.fori_loop(0, td, wait, 0)


def _dispatch(pos, h8, xs8, td):
    n = pos.shape[0]
    pos3 = pos.reshape(n // td, 1, 2 * td)
    return pl.pallas_call(
        functools.partial(_dispatch_kernel, td=td),
        grid=(n // td,),
        in_specs=[pl.BlockSpec((1, 1, 2 * td), lambda t: (t, 0, 0), memory_space=pltpu.SMEM),
                  pl.BlockSpec(memory_space=pl.ANY), pl.BlockSpec(memory_space=pl.ANY)],
        out_specs=pl.BlockSpec(memory_space=pl.ANY),
        out_shape=jax.ShapeDtypeStruct(xs8.shape, xs8.dtype),
        scratch_shapes=[pltpu.SemaphoreType.DMA(())],
        input_output_aliases={2: 0},
        compiler_params=pltpu.CompilerParams(dimension_semantics=("arbitrary",)),
        name="dispatch",
    )(pos3, h8, xs8)


def _experts_kernel(te_ref, tv_ref, tf_ref, xs_ref, wg_ref, wu_ref, wd_ref, ys_ref,
                    x2_ref, wgb_ref, wub_ref, wdb_ref, *, tm):
    del te_ref
    t = pl.program_id(0)

    @pl.when(tf_ref[t] == 1)
    def _cast_weights():
        wgb_ref[...] = wg_ref[0].astype(BF16)
        wub_ref[...] = wu_ref[0].astype(BF16)
        wdb_ref[...] = wd_ref[0].astype(BF16)

    @pl.when(tv_ref[t] == 1)
    def _compute():
        for j in range(ROW_CHUNKS):
            x2_ref[:, j * LANES:(j + 1) * LANES] = xs_ref[pl.ds(j, tm, stride=ROW_CHUNKS), :].astype(BF16)
        x = x2_ref[...]
        g = jnp.dot(x, wgb_ref[...], preferred_element_type=F32)
        u = jnp.dot(x, wub_ref[...], preferred_element_type=F32)
        act = (g * jax.nn.sigmoid(g) * u).astype(BF16)
        y = jnp.dot(act, wdb_ref[...], preferred_element_type=F32)
        for j in range(ROW_CHUNKS):
            ys_ref[pl.ds(j, tm, stride=ROW_CHUNKS), :] = y[:, j * LANES:(j + 1) * LANES]

    @pl.when(tv_ref[t] == 0)
    def _pad():
        ys_ref[...] = jnp.zeros_like(ys_ref)


def _experts(tile_e, tile_valid, tile_first, xs8, w_eg, w_eu, w_ed, tm):
    n_tiles = tile_e.shape[0]
    grid_spec = pltpu.PrefetchScalarGridSpec(
        num_scalar_prefetch=3,
        grid=(n_tiles,),
        in_specs=[
            pl.BlockSpec((tm * ROW_CHUNKS, LANES), lambda t, te, tv, tf: (t, 0)),
            pl.BlockSpec((1, D_MODEL, D_EXPERT), lambda t, te, tv, tf: (te[t], 0, 0)),
            pl.BlockSpec((1, D_MODEL, D_EXPERT), lambda t, te, tv, tf: (te[t], 0, 0)),
            pl.BlockSpec((1, D_EXPERT, D_MODEL), lambda t, te, tv, tf: (te[t], 0, 0)),
        ],
        out_specs=pl.BlockSpec((tm * ROW_CHUNKS, LANES), lambda t, te, tv, tf: (t, 0)),
        scratch_shapes=[pltpu.VMEM((tm, D_MODEL), BF16), pltpu.VMEM((D_MODEL, D_EXPERT), BF16),
                        pltpu.VMEM((D_MODEL, D_EXPERT), BF16), pltpu.VMEM((D_EXPERT, D_MODEL), BF16)],
    )
    return pl.pallas_call(
        functools.partial(_experts_kernel, tm=tm),
        grid_spec=grid_spec,
        out_shape=jax.ShapeDtypeStruct(xs8.shape, F32),
        compiler_params=pltpu.CompilerParams(
            dimension_semantics=("arbitrary",), vmem_limit_bytes=VMEM_LIMIT),
        name="experts",
    )(tile_e, tile_valid, tile_first, xs8, w_eg, w_eu, w_ed)


def _combine_kernel(pos_ref, x1_ref, wsel_ref, gfin_ref, ys_hbm, o_ref, buf_ref, sem, *, td):
    def copy(r, k):
        return pltpu.make_async_copy(_row_tile(ys_hbm, pos_ref[0, 0, 2 * r + k]), _row_tile(buf_ref.at[k], r), sem)

    def start(r, c):
        copy(r, 0).start()
        copy(r, 1).start()
        return c

    def wait(r, c):
        copy(r, 0).wait()
        copy(r, 1).wait()
        return c

    lax.fori_loop(0, td, start, 0)
    lax.fori_loop(0, td, wait, 0)

    w = wsel_ref[...]
    w0 = w[:, 0:1]
    w1 = w[:, 1:2]
    cols = []
    for j in range(ROW_CHUNKS):
        cols.append(x1_ref[:, j * LANES:(j + 1) * LANES]
                    + w0 * buf_ref[0, pl.ds(j, td, stride=ROW_CHUNKS), :]
                    + w1 * buf_ref[1, pl.ds(j, td, stride=ROW_CHUNKS), :])
    o_ref[...] = _rms(jnp.concatenate(cols, axis=1), gfin_ref[...])


def _combine_final(pos, x1, wsel, g_final, ys8, td):
    n = pos.shape[0]
    pos3 = pos.reshape(n // td, 1, 2 * td)
    return pl.pallas_call(
        functools.partial(_combine_kernel, td=td),
        grid=(n // td,),
        in_specs=[pl.BlockSpec((1, 1, 2 * td), lambda t: (t, 0, 0), memory_space=pltpu.SMEM),
                  pl.BlockSpec((td, D_MODEL), lambda t: (t, 0)),
                  pl.BlockSpec((td, ROUTER_COLS), lambda t: (t, 0)),
                  pl.BlockSpec((1, D_MODEL), lambda t: (0, 0)),
                  pl.BlockSpec(memory_space=pl.ANY)],
        out_specs=pl.BlockSpec((td, D_MODEL), lambda t: (t, 0)),
        out_shape=jax.ShapeDtypeStruct((n, D_MODEL), F32),
        scratch_shapes=[pltpu.VMEM((2, td * ROW_CHUNKS, LANES), F32), pltpu.SemaphoreType.DMA(())],
        compiler_params=pltpu.CompilerParams(
            dimension_semantics=("arbitrary",), vmem_limit_bytes=VMEM_LIMIT),
        name="combine_final",
    )(pos3, x1, wsel, g_final, ys8)


def _routing_plan(e_sel, tm, n_tiles):
    e_flat = e_sel.reshape(-1)
    onehot = (e_flat[:, None] == jnp.arange(N_EXPERTS, dtype=I32)[None, :]).astype(I32)
    csum = jnp.cumsum(onehot, axis=0)
    rank = jnp.sum(onehot * csum, axis=1) - 1
    counts = csum[-1]
    tiles_per = (counts + tm - 1) // tm
    tile_end = jnp.cumsum(tiles_per)
    tile_start = tile_end - tiles_per
    pos = (tile_start * tm)[e_flat] + rank
    total = tile_end[-1]
    t = jnp.arange(n_tiles, dtype=I32)
    tile_e = jnp.minimum(jnp.sum((t[:, None] >= tile_end[None, :]).astype(I32), axis=1), N_EXPERTS - 1)
    last_e = jnp.max(jnp.where(counts > 0, jnp.arange(N_EXPERTS, dtype=I32), 0))
    valid = t < total
    tile_e = jnp.where(valid, tile_e, last_e).astype(I32)
    first = (valid & (t == tile_start[tile_e])).astype(I32)
    return pos.reshape(-1, 2).astype(I32), tile_e, valid.astype(I32), first


def _block_diag(w):
    nb, d, _ = w.shape
    eye = jnp.eye(nb, dtype=w.dtype)
    return (eye[:, None, :, None] * w[:, :, None, :]).reshape(nb * d, nb * d)


def kernel(x_prompt, x_sample, cache_k, cache_v, state_lru_h, state_conv, page_table, g_mix, w_in, conv_w, conv_b, w_a, b_a, w_i, b_i, lam, b_sb, g_out_lru, g_out_sb, w_out, g_ffn, w_rg, b_rg, w_re, b_re, w_eg, w_eu, w_ed, g_final):
    depth = w_in.shape[0]
    assert depth == 1, "single-layer trunk"
    batch, seq, _ = x_prompt.shape
    nseq, nt, _ = x_sample.shape
    npool, page = cache_k.shape[1], cache_k.shape[2]
    n_p = batch * seq
    n_s = nseq * nt
    row = lambda a: a.reshape(1, -1)

    w_in_bf = w_in[0].astype(BF16)
    w_gates = jnp.concatenate([_block_diag(w_a[0]), _block_diag(w_i[0])], axis=1).astype(BF16)
    b_gates = jnp.concatenate([b_a[0], b_i[0]]).reshape(1, -1)
    w_out_bf = w_out[0].astype(BF16)
    w_r = jnp.zeros((D_MODEL, ROUTER_COLS), F32)
    w_r = w_r.at[:, :N_GROUPS].set(w_rg[0]).at[:, N_GROUPS:N_GROUPS + N_EXPERTS].set(w_re[0])
    wr_hi = w_r.astype(BF16)
    wr_lo = (w_r - wr_hi.astype(F32)).astype(BF16)
    b_r = jnp.zeros((1, ROUTER_COLS), F32)
    b_r = b_r.at[0, :N_GROUPS].set(b_rg[0]).at[0, N_GROUPS:N_GROUPS + N_EXPERTS].set(b_re[0])
    lru_args = (row(g_mix[0]), w_in_bf, conv_w[0], row(conv_b[0]), w_gates, b_gates, row(lam[0]),
                row(g_out_lru[0]))

    x_p = x_prompt.reshape(n_p, D_MODEL)
    q_p, k_p, v_p, kb_p, vt_p, ylru_p, ht_p, cbuf_p = _inproj_lru_prompt(x_p, *lru_args, batch, seq)
    ysb_p = _sb_prompt(b_sb[0], q_p, kb_p, vt_p, row(g_out_sb[0]), batch, seq)

    x_s = x_sample.transpose(1, 0, 2).reshape(n_s, D_MODEL)
    buf_tm = state_conv[0].transpose(1, 0, 2).reshape((CONV_W - 1) * nseq, LRU_W)
    q_s, k_s, v_s, ylru_s, ht_s, cbuf_s = _inproj_lru_sample(
        x_s, state_lru_h[0], buf_tm, *lru_args, nseq, nt)
    seq_major = lambda a, w: a.reshape(-1, nseq, w).transpose(1, 0, 2)
    q_sm, k_sm, v_sm = seq_major(q_s, SB_W), seq_major(k_s, SB_W), seq_major(v_s, SB_W)
    ysb_s = _sb_sample(page_table, b_sb[0], q_sm, k_sm, v_sm, row(g_out_sb[0]),
                       cache_k[0].reshape(npool, page, SB_W), cache_v[0].reshape(npool, page, SB_W))
    x_s2 = x_sample.reshape(n_s, D_MODEL)
    ylru_s2 = seq_major(ylru_s, LRU_W).reshape(n_s, LRU_W)
    ysb_s2 = ysb_s.reshape(n_s, SB_W).astype(BF16)

    router_args = (w_out_bf, row(g_ffn[0]), wr_hi, wr_lo, b_r)
    x1_p, h8_p, e_p, wsel_p = _outproj_router(x_p, ylru_p, ysb_p, *router_args, TM_OUT)
    x1_s, h8_s, e_s, wsel_s = _outproj_router(x_s2, ylru_s2, ysb_s2, *router_args, n_s)

    n_all = n_p + n_s
    n_tiles = -(-2 * n_all // TM_EXP) + N_EXPERTS
    e_all = jnp.concatenate([e_p[:, :2], e_s[:, :2]], axis=0)
    pos, tile_e, tile_valid, tile_first = _routing_plan(e_all, TM_EXP, n_tiles)
    pos_p, pos_s = pos[:n_p], pos[n_p:]

    xs8 = jnp.zeros((n_tiles * TM_EXP * ROW_CHUNKS, LANES), F32)
    xs8 = _dispatch(pos_p, h8_p, xs8, TD)
    xs8 = _dispatch(pos_s, h8_s, xs8, n_s)
    ys8 = _experts(tile_e, tile_valid, tile_first, xs8, w_eg[0], w_eu[0], w_ed[0], TM_EXP)

    gfin = row(g_final)
    y_p = _combine_final(pos_p, x1_p, wsel_p, gfin, ys8, TD)
    y_s = _combine_final(pos_s, x1_s, wsel_s, gfin, ys8, n_s)

    heads = lambda a, lead: a.reshape(*lead, HEADS, HEAD_DIM)
    return (
        y_p.reshape(batch, seq, D_MODEL),
        y_s.reshape(nseq, nt, D_MODEL),
        heads(k_p, (1, batch, seq)),
        heads(v_p, (1, batch, seq)),
        ht_p.reshape(1, batch, LRU_W),
        cbuf_p.reshape(1, batch, CONV_W - 1, LRU_W),
        heads(k_sm, (1, nseq, nt)),
        heads(v_sm, (1, nseq, nt)),
        ht_s.reshape(1, nseq, LRU_W),
        cbuf_s.reshape(CONV_W - 1, nseq, LRU_W).transpose(1, 0, 2)[None],
    )
```

```python
import functools
import math

import jax
import jax.numpy as jnp
from jax import lax
from jax.experimental import pallas as pl
from jax.experimental.pallas import tpu as pltpu

F32 = jnp.float32
BF16 = jnp.bfloat16
I32 = jnp.int32

D_MODEL = 1024
LRU_W = 512
SB_W = 512
HEADS = 8
HEAD_DIM = 64
CONV_W = 4
LRU_C = 8.0
EPS = 1e-6
N_GROUPS = 4
EXPERTS_PER_GROUP = 8
N_EXPERTS = 32
D_EXPERT = 512

LANES = 128
SUBLANES = 8
ROW_CHUNKS = D_MODEL // LANES

TM_PROJ = 512
TQ = 256
TM_OUT = 512
TM_EXP = 256
TD = 256
PAGES_PER_STEP = 8
ROUTER_COLS = LANES
VMEM_LIMIT = 56 * 1024 * 1024


def _softplus(z):
    return jnp.maximum(z, 0.0) + jnp.log1p(jnp.exp(-jnp.abs(z)))


def _log_beta_terms(z, accurate):
    t = jnp.exp(-jnp.abs(z))
    log_beta = jnp.minimum(z, 0.0) - (jnp.log1p(t) if accurate else jnp.log(1.0 + t))
    return log_beta - z, log_beta


def _rms(x, g):
    return x * lax.rsqrt(jnp.mean(x * x, axis=-1, keepdims=True) + EPS) * g


def _gelu_tanh(x):
    c = math.sqrt(2.0 / math.pi)
    return x * (0.5 * (1.0 + jnp.tanh(c * (x + 0.044715 * (x * x * x)))))


def _neg_expm1(x):
    u = jnp.exp(x)
    near = x > -1.0
    ratio = x / jnp.where(near & (u < 1.0), jnp.log(u), -1.0)
    return jnp.where(near, jnp.where(u < 1.0, (1.0 - u) * ratio, -x), 1.0 - u)


def _lru_gates(u, gates, lam):
    r = jax.nn.sigmoid(gates[:, :LRU_W])
    i = jax.nn.sigmoid(gates[:, LRU_W:])
    log_a = (-LRU_C) * r * _softplus(-lam)
    a = jnp.exp(log_a)
    xin = jnp.sqrt(_neg_expm1(2.0 * log_a)) * (i * u)
    return a, xin


def _split_bf16(x):
    hi = x.astype(BF16)
    lo = (x - hi.astype(F32)).astype(BF16)
    return hi, lo


def _dot3(x, w):
    x_hi, x_lo = _split_bf16(x)
    w_hi, w_lo = _split_bf16(w)
    n = x.shape[0]
    top = jnp.dot(jnp.concatenate([x_hi, x_lo], axis=0), w_hi, preferred_element_type=F32)
    return top[:n] + top[n:] + jnp.dot(x_hi, w_lo, preferred_element_type=F32)


def _inproj_lru_kernel(x_ref, gmix_ref, win_ref, cw_ref, cb_ref, wg_ref, bg_ref, lam_ref, glru_ref,
                       q_ref, k_ref, v_ref, kb_ref, vt_ref, ylru_ref, ht_ref, cbuf_ref,
                       xlp_ref, a_ref, s_ref, h_ref, *, tm, tk):
    t = pl.program_id(1)
    nt = pl.num_programs(1)

    @pl.when(t == 0)
    def _init():
        xlp_ref[0:SUBLANES, :] = jnp.zeros((SUBLANES, LRU_W), F32)
        h_ref[...] = jnp.zeros_like(h_ref)

    hn = _rms(x_ref[...], gmix_ref[...]).astype(BF16)

    def proj(c):
        return jnp.dot(hn, win_ref[:, c * LRU_W:(c + 1) * LRU_W], preferred_element_type=F32)

    xl = proj(0)
    gl = proj(1)
    q_ref[...] = proj(2).astype(BF16)
    k = proj(3)
    k_ref[...] = k
    kb_ref[...] = k.astype(BF16)
    v = proj(4)
    v_ref[...] = v
    for c in range(tm // tk):
        vt_ref[0, c] = v[c * tk:(c + 1) * tk, :].T.astype(BF16)

    xlp_ref[SUBLANES:SUBLANES + tm, :] = xl
    cw = cw_ref[...]
    u = cb_ref[...] + cw[3:4, :] * xl
    for j in range(1, CONV_W):
        u = u + cw[3 - j:4 - j, :] * xlp_ref[pl.ds(SUBLANES - j, tm), :]
    xlp_ref[0:SUBLANES, :] = xl[tm - SUBLANES:tm, :]

    gates = jnp.dot(u.astype(BF16), wg_ref[...], preferred_element_type=F32) + bg_ref[...]
    a, xin = _lru_gates(u, gates, lam_ref[...])
    a_ref[...] = a
    s_ref[...] = xin

    row = lax.broadcasted_iota(I32, (SUBLANES, LRU_W), 0)

    def group(g, h):
        r0 = pl.multiple_of(g * SUBLANES, SUBLANES)
        av = a_ref[pl.ds(r0, SUBLANES), :]
        xv = s_ref[pl.ds(r0, SUBLANES), :]
        for d in (1, 2, 4):
            valid = row >= d
            xv = jnp.where(valid, xv + av * pltpu.roll(xv, d, 0), xv)
            av = jnp.where(valid, av * pltpu.roll(av, d, 0), av)
        hv = xv + av * h
        s_ref[pl.ds(r0, SUBLANES), :] = hv
        return hv[SUBLANES - 1:SUBLANES, :]

    h = lax.fori_loop(0, tm // SUBLANES, group, h_ref[0:1, :])
    h_ref[0:1, :] = h

    y = s_ref[...] * _gelu_tanh(gl)
    ylru_ref[...] = _rms(y, glru_ref[...]).astype(BF16)

    @pl.when(t == nt - 1)
    def _fin():
        ht_ref[0] = h
        cbuf_ref[0] = xl[tm - (CONV_W - 1):tm, :]


def _inproj_lru_prompt(x2, g_mix, w_in_bf, conv_w, conv_b, w_gates, b_gates, lam, g_lru, batch, seq):
    tm, tk = TM_PROJ, TQ
    nt = seq // tm
    n = batch * seq
    row_spec = lambda w: pl.BlockSpec((tm, w), lambda b, t: (b * nt + t, 0))
    full = lambda a: pl.BlockSpec(a.shape, lambda b, t: (0,) * a.ndim)
    out_shape = (
        jax.ShapeDtypeStruct((n, SB_W), BF16),
        jax.ShapeDtypeStruct((n, SB_W), F32),
        jax.ShapeDtypeStruct((n, SB_W), F32),
        jax.ShapeDtypeStruct((n, SB_W), BF16),
        jax.ShapeDtypeStruct((batch, seq // tk, SB_W, tk), BF16),
        jax.ShapeDtypeStruct((n, LRU_W), BF16),
        jax.ShapeDtypeStruct((batch, 1, LRU_W), F32),
        jax.ShapeDtypeStruct((batch, CONV_W - 1, LRU_W), F32),
    )
    out_specs = (
        row_spec(SB_W), row_spec(SB_W), row_spec(SB_W), row_spec(SB_W),
        pl.BlockSpec((1, tm // tk, SB_W, tk), lambda b, t: (b, t, 0, 0)),
        row_spec(LRU_W),
        pl.BlockSpec((1, 1, LRU_W), lambda b, t: (b, 0, 0)),
        pl.BlockSpec((1, CONV_W - 1, LRU_W), lambda b, t: (b, 0, 0)),
    )
    args = (x2, g_mix, w_in_bf, conv_w, conv_b, w_gates, b_gates, lam, g_lru)
    return pl.pallas_call(
        functools.partial(_inproj_lru_kernel, tm=tm, tk=tk),
        grid=(batch, nt),
        in_specs=[row_spec(D_MODEL)] + [full(a) for a in args[1:]],
        out_specs=out_specs,
        out_shape=out_shape,
        scratch_shapes=[
            pltpu.VMEM((tm + SUBLANES, LRU_W), F32),
            pltpu.VMEM((tm, LRU_W), F32),
            pltpu.VMEM((tm, LRU_W), F32),
            pltpu.VMEM((SUBLANES, LRU_W), F32),
        ],
        compiler_params=pltpu.CompilerParams(
            dimension_semantics=("arbitrary", "arbitrary"), vmem_limit_bytes=VMEM_LIMIT),
        name="inproj_lru_prompt",
    )(*args)


def _inproj_lru_sample_kernel(x_ref, h0_ref, buf_ref, gmix_ref, win_ref, cw_ref, cb_ref, wg_ref, bg_ref,
                              lam_ref, glru_ref, q_ref, k_ref, v_ref, ylru_ref, ht_ref, cbuf_ref,
                              *, nseq, nt):
    hn = _rms(x_ref[...], gmix_ref[...])

    def proj(c):
        return _dot3(hn, win_ref[:, c * LRU_W:(c + 1) * LRU_W])

    xl = proj(0)
    gl = proj(1)
    q_ref[...] = proj(2)
    k_ref[...] = proj(3)
    v_ref[...] = proj(4)

    hist = [buf_ref[m * nseq:(m + 1) * nseq, :] for m in range(CONV_W - 1)]
    hist += [xl[t * nseq:(t + 1) * nseq, :] for t in range(nt)]
    cw = cw_ref[...]
    us = []
    for t in range(nt):
        ut = cb_ref[...] + cw[0:1, :] * hist[t]
        for j in range(1, CONV_W):
            ut = ut + cw[j:j + 1, :] * hist[t + j]
        us.append(ut)
    u = jnp.concatenate(us, axis=0)
    gates = _dot3(u, wg_ref[...]) + bg_ref[...]
    a, xin = _lru_gates(u, gates, lam_ref[...])
    h = h0_ref[...]
    hs = []
    for t in range(nt):
        h = a[t * nseq:(t + 1) * nseq, :] * h + xin[t * nseq:(t + 1) * nseq, :]
        hs.append(h)
    y = jnp.concatenate(hs, axis=0) * _gelu_tanh(gl)
    ylru_ref[...] = _rms(y, glru_ref[...])
    ht_ref[...] = h
    cbuf_ref[...] = jnp.concatenate(hist[nt:nt + CONV_W - 1], axis=0)


def _inproj_lru_sample(x_tm, h0, buf_tm, g_mix, w_in, conv_w, conv_b, w_gates, b_gates, lam, g_lru,
                       nseq, nt):
    n = nseq * nt
    out_shape = (
        jax.ShapeDtypeStruct((n, SB_W), F32),
        jax.ShapeDtypeStruct((n, SB_W), F32),
        jax.ShapeDtypeStruct((n, SB_W), F32),
        jax.ShapeDtypeStruct((n, LRU_W), F32),
        jax.ShapeDtypeStruct((nseq, LRU_W), F32),
        jax.ShapeDtypeStruct(((CONV_W - 1) * nseq, LRU_W), F32),
    )
    return pl.pallas_call(
        functools.partial(_inproj_lru_sample_kernel, nseq=nseq, nt=nt),
        out_shape=out_shape,
        compiler_params=pltpu.CompilerParams(vmem_limit_bytes=VMEM_LIMIT),
        name="inproj_lru_sample",
    )(x_tm, h0, buf_tm, g_mix, w_in, conv_w, conv_b, w_gates, b_gates, lam, g_lru)


def _sb_prompt_kernel(bsb_ref, q_ref, k_ref, vt_ref, g_ref, o_ref, acc_ref, c_ref, qm_ref, *, tq):
    i = pl.program_id(1)
    tk = tq
    rowk = lax.broadcasted_iota(I32, (tk, tq), 0)
    colq = lax.broadcasted_iota(I32, (tk, tq), 1)
    diag_mask = rowk < colq
    later_mat = (lax.broadcasted_iota(I32, (tk, tk), 1) > lax.broadcasted_iota(I32, (tk, tk), 0)).astype(BF16)
    lane = lax.broadcasted_iota(I32, (tq, LANES), 1)
    scale = jnp.asarray(HEAD_DIM ** -0.5, BF16)
    heads_per_tile = LANES // HEAD_DIM

    for h in range(HEADS):
        pair, half = divmod(h, heads_per_tile)
        in_head = (lane >= half * HEAD_DIM) & (lane < (half + 1) * HEAD_DIM)
        qm_ref[h] = jnp.where(in_head, q_ref[:, pair * LANES:(pair + 1) * LANES] * scale, jnp.zeros((), BF16))
    acc_ref[...] = jnp.zeros_like(acc_ref)
    c_ref[...] = jnp.zeros_like(c_ref)

    later2 = jnp.concatenate([later_mat, later_mat], axis=1)

    def block(j, masked):
        k0 = pl.multiple_of(j * tk, tk)
        kbs = [k_ref[pl.ds(k0, tk), p * LANES:(p + 1) * LANES] for p in range(HEADS // heads_per_tile)]
        ss = [lax.dot_general(kbs[h // heads_per_tile], qm_ref[h], (((1,), (1,)), ((), ())),
                              preferred_element_type=F32) + bsb_ref[h] for h in range(HEADS)]
        l1s, lbs, splits = [], [], []
        for h in range(HEADS):
            l1, lb = _log_beta_terms(ss[h], accurate=False)
            if masked:
                l1 = jnp.where(diag_mask, l1, 0.0)
            l1s.append(l1)
            lbs.append(lb)
            splits.append(jnp.concatenate(_split_bf16(l1), axis=0))
        exs = [jnp.dot(later2, splits[h], preferred_element_type=F32) for h in range(HEADS)]
        ws = []
        for h in range(HEADS):
            w = jnp.exp(lbs[h] + exs[h] + c_ref[h:h + 1, :])
            if masked:
                w = jnp.where(diag_mask, w, 0.0)
            ws.append(w.astype(BF16))
        for h in range(HEADS):
            rows = slice(h * HEAD_DIM, (h + 1) * HEAD_DIM)
            acc_ref[rows, :] += jnp.dot(vt_ref[0, j, rows, :], ws[h], preferred_element_type=F32)
            c_ref[h:h + 1, :] += exs[h][0:1, :] + l1s[h][0:1, :]

    block(i, True)

    def body(jj, carry):
        block(i - 1 - jj, False)
        return carry

    lax.fori_loop(0, i, body, 0)
    o_ref[...] = _rms(acc_ref[...].T, g_ref[...]).astype(BF16)


def _sb_prompt(b_sb, q_bf, k_bf, vt_bf, g_sb, batch, seq):
    tq = TQ
    nq = seq // tq
    n = batch * seq
    return pl.pallas_call(
        functools.partial(_sb_prompt_kernel, tq=tq),
        grid=(batch, nq),
        in_specs=[
            pl.BlockSpec(memory_space=pltpu.SMEM),
            pl.BlockSpec((tq, SB_W), lambda b, i: (b * nq + i, 0)),
            pl.BlockSpec((seq, SB_W), lambda b, i: (b, 0)),
            pl.BlockSpec((1, nq, SB_W, tq), lambda b, i: (b, 0, 0, 0)),
            pl.BlockSpec((1, SB_W), lambda b, i: (0, 0)),
        ],
        out_specs=pl.BlockSpec((tq, SB_W), lambda b, i: (b * nq + i, 0)),
        out_shape=jax.ShapeDtypeStruct((n, SB_W), BF16),
        scratch_shapes=[pltpu.VMEM((SB_W, tq), F32), pltpu.VMEM((HEADS, tq), F32),
                        pltpu.VMEM((HEADS, tq, LANES), BF16)],
        compiler_params=pltpu.CompilerParams(
            dimension_semantics=("arbitrary", "arbitrary"), vmem_limit_bytes=VMEM_LIMIT),
        name="sb_prompt",
    )(b_sb, q_bf, k_bf, vt_bf, g_sb)


def _sb_sample_kernel(pt_ref, bsb_ref, q_ref, kown_ref, vown_ref, g_ref, *rest, npg, nt, page):
    kp = rest[:npg]
    vp = rest[npg:2 * npg]
    o_ref = rest[2 * npg]
    qbd_ref, acc_ref, c_ref = rest[2 * npg + 1:]
    ci = pl.program_id(1)
    nc = pl.num_programs(1)
    rows = nt * HEADS
    rowi = lax.broadcasted_iota(I32, (rows, 1), 0)
    row_h = rowi & (HEADS - 1)
    row_t = rowi >> int(math.log2(HEADS))
    bias = jnp.zeros((rows, 1), F32)
    for h in range(HEADS):
        bias = jnp.where(row_h == h, bsb_ref[h], bias)
    head_of_lane = lax.broadcasted_iota(I32, (HEADS, SB_W), 1) // HEAD_DIM
    head_mask = head_of_lane == lax.broadcasted_iota(I32, (HEADS, SB_W), 0)

    @pl.when(ci == 0)
    def _init():
        q = q_ref[0] * (HEAD_DIM ** -0.5)
        qbd = jnp.concatenate(
            [jnp.where(head_mask, jnp.broadcast_to(q[t:t + 1, :], (HEADS, SB_W)), 0.0) for t in range(nt)],
            axis=0)
        qbd_ref[...] = jnp.concatenate(_split_bf16(qbd), axis=0)
        kown = kown_ref[0]
        vown = vown_ref[0]
        c = jnp.zeros((rows, 1), F32)
        acc = jnp.zeros((rows, SB_W), F32)
        for s in range(nt - 1, -1, -1):
            z = jnp.sum(qbd * kown[s:s + 1, :], axis=-1, keepdims=True) + bias
            vis = s < row_t
            l1, lb = _log_beta_terms(z, accurate=True)
            l1 = jnp.where(vis, l1, 0.0)
            w = jnp.where(vis, jnp.exp(lb + c), 0.0)
            acc = acc + w * vown[s:s + 1, :]
            c = c + l1
        acc_ref[...] = acc
        c_ref[...] = jnp.broadcast_to(c, (rows, LANES))

    later_mat = (lax.broadcasted_iota(I32, (page, page), 0) > lax.broadcasted_iota(I32, (page, page), 1)).astype(BF16)
    nt_dims = (((1,), (1,)), ((), ()))
    q2 = qbd_ref[...]
    q_hi = q2[:rows]

    def fold(x):
        return x[:rows] + x[rows:]

    ss = []
    for i in range(npg):
        k_hi, k_lo = _split_bf16(kp[i][0])
        ss.append(fold(lax.dot_general(q2, k_hi, nt_dims, preferred_element_type=F32))
                  + lax.dot_general(q_hi, k_lo, nt_dims, preferred_element_type=F32) + bias)
    l1s, lbs, exs = [], [], []
    for i in range(npg):
        l1, lb = _log_beta_terms(ss[i], accurate=True)
        l1s.append(l1)
        lbs.append(lb)
        exs.append(fold(jnp.dot(jnp.concatenate(_split_bf16(l1), axis=0), later_mat,
                                preferred_element_type=F32)))
    c = c_ref[...]
    acc = acc_ref[...]
    for i in range(npg):
        w = jnp.exp(lbs[i] + exs[i] + c)
        w_hi, w_lo = _split_bf16(w)
        v_hi, v_lo = _split_bf16(vp[i][0])
        acc = (acc + fold(jnp.dot(jnp.concatenate([w_hi, w_lo], axis=0), v_hi, preferred_element_type=F32))
               + jnp.dot(w_hi, v_lo, preferred_element_type=F32))
        c = c + jnp.broadcast_to(exs[i][:, 0:1] + l1s[i][:, 0:1], (rows, LANES))
    acc_ref[...] = acc
    c_ref[...] = c

    @pl.when(ci == nc - 1)
    def _fin():
        ys = [jnp.sum(jnp.where(head_mask, acc[t * HEADS:(t + 1) * HEADS, :], 0.0), axis=0, keepdims=True)
              for t in range(nt)]
        o_ref[0] = _rms(jnp.concatenate(ys, axis=0), g_ref[...])


def _sb_sample(page_table, b_sb, q_s, k_own, v_own, g_sb, cache_k, cache_v):
    nseq, npages = page_table.shape
    nt = q_s.shape[1]
    npool, page = cache_k.shape[0], cache_k.shape[1]
    npg = PAGES_PER_STEP
    nc = npages // npg
    rows = nt * HEADS

    def page_spec(i):
        return pl.BlockSpec(
            (1, page, SB_W),
            lambda s, c, pt, i=i: (pt[s * npages + (npages - 1) - (c * npg + i)], 0, 0))

    seq_spec = pl.BlockSpec((1, nt, SB_W), lambda s, c, pt: (s, 0, 0))
    grid_spec = pltpu.PrefetchScalarGridSpec(
        num_scalar_prefetch=1,
        grid=(nseq, nc),
        in_specs=[pl.BlockSpec(memory_space=pltpu.SMEM), seq_spec, seq_spec, seq_spec,
                  pl.BlockSpec((1, SB_W), lambda s, c, pt: (0, 0))]
                 + [page_spec(i) for i in range(npg)] + [page_spec(i) for i in range(npg)],
        out_specs=pl.BlockSpec((1, nt, SB_W), lambda s, c, pt: (s, 0, 0)),
        scratch_shapes=[pltpu.VMEM((2 * rows, SB_W), BF16), pltpu.VMEM((rows, SB_W), F32),
                        pltpu.VMEM((rows, LANES), F32)],
    )
    return pl.pallas_call(
        functools.partial(_sb_sample_kernel, npg=npg, nt=nt, page=page),
        grid_spec=grid_spec,
        out_shape=jax.ShapeDtypeStruct((nseq, nt, SB_W), F32),
        compiler_params=pltpu.CompilerParams(
            dimension_semantics=("arbitrary", "arbitrary"), vmem_limit_bytes=VMEM_LIMIT),
        name="sb_sample",
    )(page_table.reshape(-1), b_sb, q_s, k_own, v_own, g_sb, *([cache_k] * npg), *([cache_v] * npg))


def _outproj_router_kernel(x_ref, yl_ref, ys_ref, wout_ref, gffn_ref, wr_ref, br_ref,
                           x1_ref, h8_ref, e_ref, w_ref, *, tm, precise):
    lru_rows = slice(0, LRU_W)
    sb_rows = slice(LRU_W, LRU_W + SB_W)
    if precise:
        x1 = (x_ref[...] + _dot3(yl_ref[...], wout_ref[lru_rows, :]) + _dot3(ys_ref[...], wout_ref[sb_rows, :]))
    else:
        x1 = (x_ref[...]
              + jnp.dot(yl_ref[...], wout_ref[lru_rows, :], preferred_element_type=F32)
              + jnp.dot(ys_ref[...], wout_ref[sb_rows, :], preferred_element_type=F32))
    x1_ref[...] = x1
    h = _rms(x1, gffn_ref[...])
    for j in range(ROW_CHUNKS):
        h8_ref[pl.ds(j, tm, stride=ROW_CHUNKS), :] = h[:, j * LANES:(j + 1) * LANES]

    logits = _dot3(h, wr_ref[...]) + br_ref[...]

    lane = lax.broadcasted_iota(I32, (tm, ROUTER_COLS), 1)
    big = jnp.asarray(ROUTER_COLS, I32)
    neg = jnp.asarray(-jnp.inf, F32)
    gl = jnp.where(lane < N_GROUPS, logits, neg)
    gmax = jnp.max(gl, axis=-1, keepdims=True)
    g_idx = jnp.min(jnp.where(gl == gmax, lane, big), axis=-1, keepdims=True)
    p_g = 1.0 / jnp.sum(jnp.exp(gl - gmax), axis=-1, keepdims=True)
    e_lo = N_GROUPS + g_idx * EXPERTS_PER_GROUP
    el = jnp.where((lane >= e_lo) & (lane < e_lo + EXPERTS_PER_GROUP), logits, neg)
    m1 = jnp.max(el, axis=-1, keepdims=True)
    i1 = jnp.min(jnp.where(el == m1, lane, big), axis=-1, keepdims=True)
    el2 = jnp.where(lane == i1, neg, el)
    m2 = jnp.max(el2, axis=-1, keepdims=True)
    i2 = jnp.min(jnp.where(el2 == m2, lane, big), axis=-1, keepdims=True)
    r = jnp.exp(m2 - m1)
    w1 = p_g / (1.0 + r)
    w2 = w1 * r
    e_ref[...] = jnp.where(lane == 0, i1 - N_GROUPS, jnp.where(lane == 1, i2 - N_GROUPS, 0))
    w_ref[...] = jnp.where(lane == 0, w1, jnp.where(lane == 1, w2, 0.0))


def _outproj_router(x2, ylru, ysb, w_out, g_ffn, w_r, b_r, tm):
    n = x2.shape[0]
    precise = ylru.dtype == F32
    assert ysb.dtype == ylru.dtype == w_out.dtype
    row_spec = lambda w: pl.BlockSpec((tm, w), lambda t: (t, 0))
    full = lambda a: pl.BlockSpec(a.shape, lambda t: (0,) * a.ndim)
    return pl.pallas_call(
        functools.partial(_outproj_router_kernel, tm=tm, precise=precise),
        grid=(n // tm,),
        in_specs=[row_spec(D_MODEL), row_spec(LRU_W), row_spec(SB_W), full(w_out), full(g_ffn), full(w_r),
                  full(b_r)],
        out_specs=(row_spec(D_MODEL), pl.BlockSpec((tm * ROW_CHUNKS, LANES), lambda t: (t, 0)),
                   row_spec(ROUTER_COLS), row_spec(ROUTER_COLS)),
        out_shape=(jax.ShapeDtypeStruct((n, D_MODEL), F32),
                   jax.ShapeDtypeStruct((n * ROW_CHUNKS, LANES), F32),
                   jax.ShapeDtypeStruct((n, ROUTER_COLS), I32),
                   jax.ShapeDtypeStruct((n, ROUTER_COLS), F32)),
        compiler_params=pltpu.CompilerParams(
            dimension_semantics=("arbitrary",), vmem_limit_bytes=VMEM_LIMIT),
        name="outproj_router",
    )(x2, ylru, ysb, w_out, g_ffn, w_r, b_r)


def _row_tile(ref, row):
    return ref.at[pl.ds(pl.multiple_of(row * ROW_CHUNKS, ROW_CHUNKS), ROW_CHUNKS)]


def _dispatch_kernel(pos_ref, h8_ref, xs_in_hbm, xs_hbm, sem, *, td):
    del xs_in_hbm

    def copy(r, k):
        return pltpu.make_async_copy(_row_tile(h8_ref, r), _row_tile(xs_hbm, pos_ref[0, 0, 2 * r + k]), sem)

    def start(r, c):
        copy(r, 0).start()
        copy(r, 1).start()
        return c

    def wait(r, c):
        copy(r, 0).wait()
        copy(r, 1).wait()
        return c

    lax.fori_loop(0, td, start, 0)
    lax.fori_loop(0, td, wait, 0)


def _dispatch(pos, h8, xs8, td):
    n = pos.shape[0]
    pos3 = pos.reshape(n // td, 1, 2 * td)
    return pl.pallas_call(
        functools.partial(_dispatch_kernel, td=td),
        grid=(n // td,),
        in_specs=[pl.BlockSpec((1, 1, 2 * td), lambda t: (t, 0, 0), memory_space=pltpu.SMEM),
                  pl.BlockSpec((td * ROW_CHUNKS, LANES), lambda t: (t, 0)),
                  pl.BlockSpec(memory_space=pl.ANY)],
        out_specs=pl.BlockSpec(memory_space=pl.ANY),
        out_shape=jax.ShapeDtypeStruct(xs8.shape, xs8.dtype),
        scratch_shapes=[pltpu.SemaphoreType.DMA(())],
        input_output_aliases={2: 0},
        compiler_params=pltpu.CompilerParams(dimension_semantics=("arbitrary",)),
        name="dispatch",
    )(pos3, h8, xs8)


def _experts_kernel(te_ref, tv_ref, tf_ref, xs_ref, wg_ref, wu_ref, wd_ref, ys_ref,
                    x2_ref, wgb_ref, wub_ref, wdb_ref, *, tm):
    del te_ref
    t = pl.program_id(0)

    @pl.when(tf_ref[t] == 1)
    def _cast_weights():
        wgb_ref[...] = wg_ref[0].astype(BF16)
        wub_ref[...] = wu_ref[0].astype(BF16)
        wdb_ref[...] = wd_ref[0].astype(BF16)

    @pl.when(tv_ref[t] == 1)
    def _compute():
        for j in range(ROW_CHUNKS):
            x2_ref[:, j * LANES:(j + 1) * LANES] = xs_ref[pl.ds(j, tm, stride=ROW_CHUNKS), :].astype(BF16)
        x = x2_ref[...]
        g = jnp.dot(x, wgb_ref[...], preferred_element_type=F32)
        u = jnp.dot(x, wub_ref[...], preferred_element_type=F32)
        act = (g * jax.nn.sigmoid(g) * u).astype(BF16)
        y = jnp.dot(act, wdb_ref[...], preferred_element_type=F32)
        for j in range(ROW_CHUNKS):
            ys_ref[pl.ds(j, tm, stride=ROW_CHUNKS), :] = y[:, j * LANES:(j + 1) * LANES]

    @pl.when(tv_ref[t] == 0)
    def _pad():
        ys_ref[...] = jnp.zeros_like(ys_ref)


def _experts(tile_e, tile_valid, tile_first, xs8, w_eg, w_eu, w_ed, tm):
    n_tiles = tile_e.shape[0]
    grid_spec = pltpu.PrefetchScalarGridSpec(
        num_scalar_prefetch=3,
        grid=(n_tiles,),
        in_specs=[
            pl.BlockSpec((tm * ROW_CHUNKS, LANES), lambda t, te, tv, tf: (t, 0)),
            pl.BlockSpec((1, D_MODEL, D_EXPERT), lambda t, te, tv, tf: (te[t], 0, 0)),
            pl.BlockSpec((1, D_MODEL, D_EXPERT), lambda t, te, tv, tf: (te[t], 0, 0)),
            pl.BlockSpec((1, D_EXPERT, D_MODEL), lambda t, te, tv, tf: (te[t], 0, 0)),
        ],
        out_specs=pl.BlockSpec((tm * ROW_CHUNKS, LANES), lambda t, te, tv, tf: (t, 0)),
        scratch_shapes=[pltpu.VMEM((tm, D_MODEL), BF16), pltpu.VMEM((D_MODEL, D_EXPERT), BF16),
                        pltpu.VMEM((D_MODEL, D_EXPERT), BF16), pltpu.VMEM((D_EXPERT, D_MODEL), BF16)],
    )
    return pl.pallas_call(
        functools.partial(_experts_kernel, tm=tm),
        grid_spec=grid_spec,
        out_shape=jax.ShapeDtypeStruct(xs8.shape, F32),
        compiler_params=pltpu.CompilerParams(
            dimension_semantics=("arbitrary",), vmem_limit_bytes=VMEM_LIMIT),
        name="experts",
    )(tile_e, tile_valid, tile_first, xs8, w_eg, w_eu, w_ed)


def _combine_kernel(pos_ref, x1_ref, wsel_ref, gfin_ref, ys_hbm, o_ref, buf_ref, sem, *, td):
    def copy(r, k):
        return pltpu.make_async_copy(_row_tile(ys_hbm, pos_ref[0, 0, 2 * r + k]), _row_tile(buf_ref.at[k], r), sem)

    def start(r, c):
        copy(r, 0).start()
        copy(r, 1).start()
        return c

    def wait(r, c):
        copy(r, 0).wait()
        copy(r, 1).wait()
        return c

    lax.fori_loop(0, td, start, 0)
    lax.fori_loop(0, td, wait, 0)

    w = wsel_ref[...]
    w0 = w[:, 0:1]
    w1 = w[:, 1:2]
    cols = []
    for j in range(ROW_CHUNKS):
        cols.append(x1_ref[:, j * LANES:(j + 1) * LANES]
                    + w0 * buf_ref[0, pl.ds(j, td, stride=ROW_CHUNKS), :]
                    + w1 * buf_ref[1, pl.ds(j, td, stride=ROW_CHUNKS), :])
    o_ref[...] = _rms(jnp.concatenate(cols, axis=1), gfin_ref[...])


def _combine_final(pos, x1, wsel, g_final, ys8, td):
    n = pos.shape[0]
    pos3 = pos.reshape(n // td, 1, 2 * td)
    return pl.pallas_call(
        functools.partial(_combine_kernel, td=td),
        grid=(n // td,),
        in_specs=[pl.BlockSpec((1, 1, 2 * td), lambda t: (t, 0, 0), memory_space=pltpu.SMEM),
                  pl.BlockSpec((td, D_MODEL), lambda t: (t, 0)),
                  pl.BlockSpec((td, ROUTER_COLS), lambda t: (t, 0)),
                  pl.BlockSpec((1, D_MODEL), lambda t: (0, 0)),
                  pl.BlockSpec(memory_space=pl.ANY)],
        out_specs=pl.BlockSpec((td, D_MODEL), lambda t: (t, 0)),
        out_shape=jax.ShapeDtypeStruct((n, D_MODEL), F32),
        scratch_shapes=[pltpu.VMEM((2, td * ROW_CHUNKS, LANES), F32), pltpu.SemaphoreType.DMA(())],
        compiler_params=pltpu.CompilerParams(
            dimension_semantics=("arbitrary",), vmem_limit_bytes=VMEM_LIMIT),
        name="combine_final",
    )(pos3, x1, wsel, g_final, ys8)


def _routing_plan(e_sel, tm, n_tiles):
    e_flat = e_sel.reshape(-1)
    onehot = (e_flat[:, None] == jnp.arange(N_EXPERTS, dtype=I32)[None, :]).astype(I32)
    csum = jnp.cumsum(onehot, axis=0)
    rank = jnp.sum(onehot * csum, axis=1) - 1
    counts = csum[-1]
    tiles_per = (counts + tm - 1) // tm
    tile_end = jnp.cumsum(tiles_per)
    tile_start = tile_end - tiles_per
    pos = (tile_start * tm)[e_flat] + rank
    total = tile_end[-1]
    t = jnp.arange(n_tiles, dtype=I32)
    tile_e = jnp.minimum(jnp.sum((t[:, None] >= tile_end[None, :]).astype(I32), axis=1), N_EXPERTS - 1)
    last_e = jnp.max(jnp.where(counts > 0, jnp.arange(N_EXPERTS, dtype=I32), 0))
    valid = t < total
    tile_e = jnp.where(valid, tile_e, last_e).astype(I32)
    first = (valid & (t == tile_start[tile_e])).astype(I32)
    return pos.reshape(-1, 2).astype(I32), tile_e, valid.astype(I32), first


def _block_diag(w):
    nb, d, _ = w.shape
    eye = jnp.eye(nb, dtype=w.dtype)
    return (eye[:, None, :, None] * w[:, :, None, :]).reshape(nb * d, nb * d)


def kernel(x_prompt, x_sample, cache_k, cache_v, state_lru_h, state_conv, page_table, g_mix, w_in, conv_w, conv_b, w_a, b_a, w_i, b_i, lam, b_sb, g_out_lru, g_out_sb, w_out, g_ffn, w_rg, b_rg, w_re, b_re, w_eg, w_eu, w_ed, g_final):
    depth = w_in.shape[0]
    assert depth == 1, "single-layer trunk"
    batch, seq, _ = x_prompt.shape
    nseq, nt, _ = x_sample.shape
    npool, page = cache_k.shape[1], cache_k.shape[2]
    n_p = batch * seq
    n_s = nseq * nt
    row = lambda a: a.reshape(1, -1)

    w_gates = jnp.concatenate([_block_diag(w_a[0]), _block_diag(w_i[0])], axis=1)
    b_gates = jnp.concatenate([b_a[0], b_i[0]]).reshape(1, -1)
    w_r = jnp.zeros((D_MODEL, ROUTER_COLS), F32)
    w_r = w_r.at[:, :N_GROUPS].set(w_rg[0]).at[:, N_GROUPS:N_GROUPS + N_EXPERTS].set(w_re[0])
    b_r = jnp.zeros((1, ROUTER_COLS), F32)
    b_r = b_r.at[0, :N_GROUPS].set(b_rg[0]).at[0, N_GROUPS:N_GROUPS + N_EXPERTS].set(b_re[0])
    lru_tail = (row(lam[0]), row(g_out_lru[0]))

    x_p = x_prompt.reshape(n_p, D_MODEL)
    q_p, k_p, v_p, kb_p, vt_p, ylru_p, ht_p, cbuf_p = _inproj_lru_prompt(
        x_p, row(g_mix[0]), w_in[0].astype(BF16), conv_w[0], row(conv_b[0]), w_gates.astype(BF16), b_gates,
        *lru_tail, batch, seq)
    ysb_p = _sb_prompt(b_sb[0], q_p, kb_p, vt_p, row(g_out_sb[0]), batch, seq)

    x_s = x_sample.transpose(1, 0, 2).reshape(n_s, D_MODEL)
    buf_tm = state_conv[0].transpose(1, 0, 2).reshape((CONV_W - 1) * nseq, LRU_W)
    q_s, k_s, v_s, ylru_s, ht_s, cbuf_s = _inproj_lru_sample(
        x_s, state_lru_h[0], buf_tm, row(g_mix[0]), w_in[0], conv_w[0], row(conv_b[0]), w_gates, b_gates,
        *lru_tail, nseq, nt)
    seq_major = lambda a, w: a.reshape(-1, nseq, w).transpose(1, 0, 2)
    q_sm, k_sm, v_sm = seq_major(q_s, SB_W), seq_major(k_s, SB_W), seq_major(v_s, SB_W)
    ysb_s = _sb_sample(page_table, b_sb[0], q_sm, k_sm, v_sm, row(g_out_sb[0]),
                       cache_k[0].reshape(npool, page, SB_W), cache_v[0].reshape(npool, page, SB_W))
    x_s2 = x_sample.reshape(n_s, D_MODEL)
    ylru_s2 = seq_major(ylru_s, LRU_W).reshape(n_s, LRU_W)
    ysb_s2 = ysb_s.reshape(n_s, SB_W)

    router_args = (row(g_ffn[0]), w_r, b_r)
    x1_p, h8_p, e_p, wsel_p = _outproj_router(x_p, ylru_p, ysb_p, w_out[0].astype(BF16), *router_args, TM_OUT)
    x1_s, h8_s, e_s, wsel_s = _outproj_router(x_s2, ylru_s2, ysb_s2, w_out[0], *router_args, n_s)

    n_all = n_p + n_s
    n_tiles = -(-2 * n_all // TM_EXP) + N_EXPERTS
    e_all = jnp.concatenate([e_p[:, :2], e_s[:, :2]], axis=0)
    pos, tile_e, tile_valid, tile_first = _routing_plan(e_all, TM_EXP, n_tiles)
    pos_p, pos_s = pos[:n_p], pos[n_p:]

    xs8 = jnp.zeros((n_tiles * TM_EXP * ROW_CHUNKS, LANES), F32)
    xs8 = _dispatch(pos_p, h8_p, xs8, TD)
    xs8 = _dispatch(pos_s, h8_s, xs8, n_s)
    ys8 = _experts(tile_e, tile_valid, tile_first, xs8, w_eg[0], w_eu[0], w_ed[0], TM_EXP)

    gfin = row(g_final)
    y_p = _combine_final(pos_p, x1_p, wsel_p, gfin, ys8, TD)
    y_s = _combine_final(pos_s, x1_s, wsel_s, gfin, ys8, n_s)

    heads = lambda a, lead: a.reshape(*lead, HEADS, HEAD_DIM)
    return (
        y_p.reshape(batch, seq, D_MODEL),
        y_s.reshape(nseq, nt, D_MODEL),
        heads(k_p, (1, batch, seq)),
        heads(v_p, (1, batch, seq)),
        ht_p.reshape(1, batch, LRU_W),
        cbuf_p.reshape(1, batch, CONV_W - 1, LRU_W),
        heads(k_sm, (1, nseq, nt)),
        heads(v_sm, (1, nseq, nt)),
        ht_s.reshape(1, nseq, LRU_W),
        cbuf_s.reshape(CONV_W - 1, nseq, LRU_W).transpose(1, 0, 2)[None],
    )
```

```python
import functools
import math

import jax
import jax.numpy as jnp
from jax import lax
from jax.experimental import pallas as pl
from jax.experimental.pallas import tpu as pltpu

F32 = jnp.float32
BF16 = jnp.bfloat16
I32 = jnp.int32

D_MODEL = 1024
LRU_W = 512
SB_W = 512
HEADS = 8
HEAD_DIM = 64
CONV_W = 4
LRU_C = 8.0
EPS = 1e-6
N_GROUPS = 4
EXPERTS_PER_GROUP = 8
N_EXPERTS = 32
D_EXPERT = 512

LANES = 128
SUBLANES = 8
ROW_CHUNKS = D_MODEL // LANES

TM_PROJ = 512
TQ = 256
TM_OUT = 512
TM_EXP = 256
TD = 256
PAGES_PER_STEP = 8
ROUTER_COLS = LANES
VMEM_LIMIT = 56 * 1024 * 1024


def _softplus(z):
    return jnp.maximum(z, 0.0) + jnp.log1p(jnp.exp(-jnp.abs(z)))


def _log_beta_terms(z, accurate):
    t = jnp.exp(-jnp.abs(z))
    log_beta = jnp.minimum(z, 0.0) - (jnp.log1p(t) if accurate else jnp.log(1.0 + t))
    return log_beta - z, log_beta


def _rms(x, g):
    return x * lax.rsqrt(jnp.mean(x * x, axis=-1, keepdims=True) + EPS) * g


def _gelu_tanh(x):
    c = math.sqrt(2.0 / math.pi)
    return x * (0.5 * (1.0 + jnp.tanh(c * (x + 0.044715 * (x * x * x)))))


def _neg_expm1(x):
    u = jnp.exp(x)
    near = x > -1.0
    ratio = x / jnp.where(near & (u < 1.0), jnp.log(u), -1.0)
    return jnp.where(near, jnp.where(u < 1.0, (1.0 - u) * ratio, -x), 1.0 - u)


def _lru_gates(u, gates, lam):
    r = jax.nn.sigmoid(gates[:, :LRU_W])
    i = jax.nn.sigmoid(gates[:, LRU_W:])
    log_a = (-LRU_C) * r * _softplus(-lam)
    a = jnp.exp(log_a)
    xin = jnp.sqrt(_neg_expm1(2.0 * log_a)) * (i * u)
    return a, xin


def _split_bf16(x):
    hi = x.astype(BF16)
    lo = (x - hi.astype(F32)).astype(BF16)
    return hi, lo


def _dot3(x, w):
    x_hi, x_lo = _split_bf16(x)
    w_hi, w_lo = _split_bf16(w)
    n = x.shape[0]
    top = jnp.dot(jnp.concatenate([x_hi, x_lo], axis=0), w_hi, preferred_element_type=F32)
    return top[:n] + top[n:] + jnp.dot(x_hi, w_lo, preferred_element_type=F32)


def _inproj_lru_kernel(x_ref, gmix_ref, win_ref, cw_ref, cb_ref, wg_ref, bg_ref, lam_ref, glru_ref,
                       q_ref, k_ref, v_ref, kb_ref, vt_ref, ylru_ref, ht_ref, cbuf_ref,
                       xlp_ref, a_ref, s_ref, h_ref, *, tm, tk):
    t = pl.program_id(1)
    nt = pl.num_programs(1)

    @pl.when(t == 0)
    def _init():
        xlp_ref[0:SUBLANES, :] = jnp.zeros((SUBLANES, LRU_W), F32)
        h_ref[...] = jnp.zeros_like(h_ref)

    hn = _rms(x_ref[...], gmix_ref[...]).astype(BF16)

    def proj(c):
        return jnp.dot(hn, win_ref[:, c * LRU_W:(c + 1) * LRU_W], preferred_element_type=F32)

    xl = proj(0)
    gl = proj(1)
    q_ref[...] = proj(2).astype(BF16)
    k = proj(3)
    k_ref[...] = k
    kb_ref[...] = k.astype(BF16)
    v = proj(4)
    v_ref[...] = v
    for c in range(tm // tk):
        vt_ref[0, c] = v[c * tk:(c + 1) * tk, :].T.astype(BF16)

    xlp_ref[SUBLANES:SUBLANES + tm, :] = xl
    cw = cw_ref[...]
    u = cb_ref[...] + cw[3:4, :] * xl
    for j in range(1, CONV_W):
        u = u + cw[3 - j:4 - j, :] * xlp_ref[pl.ds(SUBLANES - j, tm), :]
    xlp_ref[0:SUBLANES, :] = xl[tm - SUBLANES:tm, :]

    gates = jnp.dot(u.astype(BF16), wg_ref[...], preferred_element_type=F32) + bg_ref[...]
    a, xin = _lru_gates(u, gates, lam_ref[...])
    a_ref[...] = a
    s_ref[...] = xin

    row = lax.broadcasted_iota(I32, (SUBLANES, LRU_W), 0)

    def group(g, h):
        r0 = pl.multiple_of(g * SUBLANES, SUBLANES)
        av = a_ref[pl.ds(r0, SUBLANES), :]
        xv = s_ref[pl.ds(r0, SUBLANES), :]
        for d in (1, 2, 4):
            valid = row >= d
            xv = jnp.where(valid, xv + av * pltpu.roll(xv, d, 0), xv)
            av = jnp.where(valid, av * pltpu.roll(av, d, 0), av)
        hv = xv + av * h
        s_ref[pl.ds(r0, SUBLANES), :] = hv
        return hv[SUBLANES - 1:SUBLANES, :]

    h = lax.fori_loop(0, tm // SUBLANES, group, h_ref[0:1, :])
    h_ref[0:1, :] = h

    y = s_ref[...] * _gelu_tanh(gl)
    ylru_ref[...] = _rms(y, glru_ref[...]).astype(BF16)

    @pl.when(t == nt - 1)
    def _fin():
        ht_ref[0] = h
        cbuf_ref[0] = xl[tm - (CONV_W - 1):tm, :]


def _inproj_lru_prompt(x2, g_mix, w_in_bf, conv_w, conv_b, w_gates, b_gates, lam, g_lru, batch, seq):
    tm, tk = TM_PROJ, TQ
    nt = seq // tm
    n = batch * seq
    row_spec = lambda w: pl.BlockSpec((tm, w), lambda b, t: (b * nt + t, 0))
    full = lambda a: pl.BlockSpec(a.shape, lambda b, t: (0,) * a.ndim)
    out_shape = (
        jax.ShapeDtypeStruct((n, SB_W), BF16),
        jax.ShapeDtypeStruct((n, SB_W), F32),
        jax.ShapeDtypeStruct((n, SB_W), F32),
        jax.ShapeDtypeStruct((n, SB_W), BF16),
        jax.ShapeDtypeStruct((batch, seq // tk, SB_W, tk), BF16),
        jax.ShapeDtypeStruct((n, LRU_W), BF16),
        jax.ShapeDtypeStruct((batch, 1, LRU_W), F32),
        jax.ShapeDtypeStruct((batch, CONV_W - 1, LRU_W), F32),
    )
    out_specs = (
        row_spec(SB_W), row_spec(SB_W), row_spec(SB_W), row_spec(SB_W),
        pl.BlockSpec((1, tm // tk, SB_W, tk), lambda b, t: (b, t, 0, 0)),
        row_spec(LRU_W),
        pl.BlockSpec((1, 1, LRU_W), lambda b, t: (b, 0, 0)),
        pl.BlockSpec((1, CONV_W - 1, LRU_W), lambda b, t: (b, 0, 0)),
    )
    args = (x2, g_mix, w_in_bf, conv_w, conv_b, w_gates, b_gates, lam, g_lru)
    return pl.pallas_call(
        functools.partial(_inproj_lru_kernel, tm=tm, tk=tk),
        grid=(batch, nt),
        in_specs=[row_spec(D_MODEL)] + [full(a) for a in args[1:]],
        out_specs=out_specs,
        out_shape=out_shape,
        scratch_shapes=[
            pltpu.VMEM((tm + SUBLANES, LRU_W), F32),
            pltpu.VMEM((tm, LRU_W), F32),
            pltpu.VMEM((tm, LRU_W), F32),
            pltpu.VMEM((SUBLANES, LRU_W), F32),
        ],
        compiler_params=pltpu.CompilerParams(
            dimension_semantics=("arbitrary", "arbitrary"), vmem_limit_bytes=VMEM_LIMIT),
        name="inproj_lru_prompt",
    )(*args)


def _inproj_lru_sample_kernel(x_ref, h0_ref, buf_ref, gmix_ref, win_ref, cw_ref, cb_ref, wg_ref, bg_ref,
                              lam_ref, glru_ref, q_ref, k_ref, v_ref, ylru_ref, ht_ref, cbuf_ref,
                              *, nseq, nt):
    hn = _rms(x_ref[...], gmix_ref[...])

    def proj(c):
        return _dot3(hn, win_ref[:, c * LRU_W:(c + 1) * LRU_W])

    xl = proj(0)
    gl = proj(1)
    q_ref[...] = proj(2)
    k_ref[...] = proj(3)
    v_ref[...] = proj(4)

    hist = [buf_ref[m * nseq:(m + 1) * nseq, :] for m in range(CONV_W - 1)]
    hist += [xl[t * nseq:(t + 1) * nseq, :] for t in range(nt)]
    cw = cw_ref[...]
    us = []
    for t in range(nt):
        ut = cb_ref[...] + cw[0:1, :] * hist[t]
        for j in range(1, CONV_W):
            ut = ut + cw[j:j + 1, :] * hist[t + j]
        us.append(ut)
    u = jnp.concatenate(us, axis=0)
    gates = _dot3(u, wg_ref[...]) + bg_ref[...]
    a, xin = _lru_gates(u, gates, lam_ref[...])
    h = h0_ref[...]
    hs = []
    for t in range(nt):
        h = a[t * nseq:(t + 1) * nseq, :] * h + xin[t * nseq:(t + 1) * nseq, :]
        hs.append(h)
    y = jnp.concatenate(hs, axis=0) * _gelu_tanh(gl)
    ylru_ref[...] = _rms(y, glru_ref[...])
    ht_ref[...] = h
    cbuf_ref[...] = jnp.concatenate(hist[nt:nt + CONV_W - 1], axis=0)


def _inproj_lru_sample(x_tm, h0, buf_tm, g_mix, w_in, conv_w, conv_b, w_gates, b_gates, lam, g_lru,
                       nseq, nt):
    n = nseq * nt
    out_shape = (
        jax.ShapeDtypeStruct((n, SB_W), F32),
        jax.ShapeDtypeStruct((n, SB_W), F32),
        jax.ShapeDtypeStruct((n, SB_W), F32),
        jax.ShapeDtypeStruct((n, LRU_W), F32),
        jax.ShapeDtypeStruct((nseq, LRU_W), F32),
        jax.ShapeDtypeStruct(((CONV_W - 1) * nseq, LRU_W), F32),
    )
    return pl.pallas_call(
        functools.partial(_inproj_lru_sample_kernel, nseq=nseq, nt=nt),
        out_shape=out_shape,
        compiler_params=pltpu.CompilerParams(vmem_limit_bytes=VMEM_LIMIT),
        name="inproj_lru_sample",
    )(x_tm, h0, buf_tm, g_mix, w_in, conv_w, conv_b, w_gates, b_gates, lam, g_lru)


def _sb_prompt_kernel(bsb_ref, q_ref, k_ref, vt_ref, g_ref, o_ref, acc_ref, c_ref, qm_ref, *, tq):
    i = pl.program_id(1)
    tk = tq
    rowk = lax.broadcasted_iota(I32, (tk, tq), 0)
    colq = lax.broadcasted_iota(I32, (tk, tq), 1)
    diag_mask = rowk < colq
    later_mat = (lax.broadcasted_iota(I32, (tk, tk), 1) > lax.broadcasted_iota(I32, (tk, tk), 0)).astype(BF16)
    lane = lax.broadcasted_iota(I32, (tq, LANES), 1)
    scale = jnp.asarray(HEAD_DIM ** -0.5, BF16)
    heads_per_tile = LANES // HEAD_DIM

    for h in range(HEADS):
        pair, half = divmod(h, heads_per_tile)
        in_head = (lane >= half * HEAD_DIM) & (lane < (half + 1) * HEAD_DIM)
        qm_ref[h] = jnp.where(in_head, q_ref[:, pair * LANES:(pair + 1) * LANES] * scale, jnp.zeros((), BF16))
    acc_ref[...] = jnp.zeros_like(acc_ref)
    c_ref[...] = jnp.zeros_like(c_ref)

    later2 = jnp.concatenate([later_mat, later_mat], axis=1)

    def block(j, masked):
        k0 = pl.multiple_of(j * tk, tk)
        kbs = [k_ref[pl.ds(k0, tk), p * LANES:(p + 1) * LANES] for p in range(HEADS // heads_per_tile)]
        ss = [lax.dot_general(kbs[h // heads_per_tile], qm_ref[h], (((1,), (1,)), ((), ())),
                              preferred_element_type=F32) + bsb_ref[h] for h in range(HEADS)]
        l1s, lbs, splits = [], [], []
        for h in range(HEADS):
            l1, lb = _log_beta_terms(ss[h], accurate=False)
            if masked:
                l1 = jnp.where(diag_mask, l1, 0.0)
            l1s.append(l1)
            lbs.append(lb)
            splits.append(jnp.concatenate(_split_bf16(l1), axis=0))
        exs = [jnp.dot(later2, splits[h], preferred_element_type=F32) for h in range(HEADS)]
        ws = []
        for h in range(HEADS):
            w = jnp.exp(lbs[h] + exs[h] + c_ref[h:h + 1, :])
            if masked:
                w = jnp.where(diag_mask, w, 0.0)
            ws.append(w.astype(BF16))
        for h in range(HEADS):
            rows = slice(h * HEAD_DIM, (h + 1) * HEAD_DIM)
            acc_ref[rows, :] += jnp.dot(vt_ref[0, j, rows, :], ws[h], preferred_element_type=F32)
            c_ref[h:h + 1, :] += exs[h][0:1, :] + l1s[h][0:1, :]

    block(i, True)

    def body(jj, carry):
        block(i - 1 - jj, False)
        return carry

    lax.fori_loop(0, i, body, 0)
    o_ref[...] = _rms(acc_ref[...].T, g_ref[...]).astype(BF16)


def _sb_prompt(b_sb, q_bf, k_bf, vt_bf, g_sb, batch, seq):
    tq = TQ
    nq = seq // tq
    n = batch * seq
    return pl.pallas_call(
        functools.partial(_sb_prompt_kernel, tq=tq),
        grid=(batch, nq),
        in_specs=[
            pl.BlockSpec(memory_space=pltpu.SMEM),
            pl.BlockSpec((tq, SB_W), lambda b, i: (b * nq + i, 0)),
            pl.BlockSpec((seq, SB_W), lambda b, i: (b, 0)),
            pl.BlockSpec((1, nq, SB_W, tq), lambda b, i: (b, 0, 0, 0)),
            pl.BlockSpec((1, SB_W), lambda b, i: (0, 0)),
        ],
        out_specs=pl.BlockSpec((tq, SB_W), lambda b, i: (b * nq + i, 0)),
        out_shape=jax.ShapeDtypeStruct((n, SB_W), BF16),
        scratch_shapes=[pltpu.VMEM((SB_W, tq), F32), pltpu.VMEM((HEADS, tq), F32),
                        pltpu.VMEM((HEADS, tq, LANES), BF16)],
        compiler_params=pltpu.CompilerParams(
            dimension_semantics=("arbitrary", "arbitrary"), vmem_limit_bytes=VMEM_LIMIT),
        name="sb_prompt",
    )(b_sb, q_bf, k_bf, vt_bf, g_sb)


def _sb_sample_kernel(pt_ref, bsb_ref, q_ref, kown_ref, vown_ref, g_ref, *rest, npg, nt, page):
    kp = rest[:npg]
    vp = rest[npg:2 * npg]
    o_ref = rest[2 * npg]
    qbd_ref, acc_ref, c_ref = rest[2 * npg + 1:]
    ci = pl.program_id(1)
    nc = pl.num_programs(1)
    rows = nt * HEADS
    rowi = lax.broadcasted_iota(I32, (rows, 1), 0)
    row_h = rowi & (HEADS - 1)
    row_t = rowi >> int(math.log2(HEADS))
    bias = jnp.zeros((rows, 1), F32)
    for h in range(HEADS):
        bias = jnp.where(row_h == h, bsb_ref[h], bias)
    head_of_lane = lax.broadcasted_iota(I32, (HEADS, SB_W), 1) // HEAD_DIM
    head_mask = head_of_lane == lax.broadcasted_iota(I32, (HEADS, SB_W), 0)

    @pl.when(ci == 0)
    def _init():
        q = q_ref[0] * (HEAD_DIM ** -0.5)
        qbd = jnp.concatenate(
            [jnp.where(head_mask, jnp.broadcast_to(q[t:t + 1, :], (HEADS, SB_W)), 0.0) for t in range(nt)],
            axis=0)
        qbd_ref[...] = jnp.concatenate(_split_bf16(qbd), axis=0)
        kown = kown_ref[0]
        vown = vown_ref[0]
        c = jnp.zeros((rows, 1), F32)
        acc = jnp.zeros((rows, SB_W), F32)
        for s in range(nt - 1, -1, -1):
            z = jnp.sum(qbd * kown[s:s + 1, :], axis=-1, keepdims=True) + bias
            vis = s < row_t
            l1, lb = _log_beta_terms(z, accurate=True)
            l1 = jnp.where(vis, l1, 0.0)
            w = jnp.where(vis, jnp.exp(lb + c), 0.0)
            acc = acc + w * vown[s:s + 1, :]
            c = c + l1
        acc_ref[...] = acc
        c_ref[...] = jnp.broadcast_to(c, (rows, LANES))

    later_mat = (lax.broadcasted_iota(I32, (page, page), 0) > lax.broadcasted_iota(I32, (page, page), 1)).astype(BF16)
    nt_dims = (((1,), (1,)), ((), ()))
    q2 = qbd_ref[...]
    q_hi = q2[:rows]

    def fold(x):
        return x[:rows] + x[rows:]

    ss = []
    for i in range(npg):
        kt_hi, kt_lo = _split_bf16(kp[i][0])
        ss.append(fold(jnp.dot(q2, kt_hi, preferred_element_type=F32))
                  + jnp.dot(q_hi, kt_lo, preferred_element_type=F32) + bias)
    l1s, lbs, exs = [], [], []
    for i in range(npg):
        l1, lb = _log_beta_terms(ss[i], accurate=True)
        l1s.append(l1)
        lbs.append(lb)
        exs.append(fold(jnp.dot(jnp.concatenate(_split_bf16(l1), axis=0), later_mat,
                                preferred_element_type=F32)))
    c = c_ref[...]
    acc = acc_ref[...]
    for i in range(npg):
        w = jnp.exp(lbs[i] + exs[i] + c)
        w_hi, w_lo = _split_bf16(w)
        vt_hi, vt_lo = _split_bf16(vp[i][0])
        acc = (acc + fold(lax.dot_general(jnp.concatenate([w_hi, w_lo], axis=0), vt_hi, nt_dims,
                                          preferred_element_type=F32))
               + lax.dot_general(w_hi, vt_lo, nt_dims, preferred_element_type=F32))
        c = c + jnp.broadcast_to(exs[i][:, 0:1] + l1s[i][:, 0:1], (rows, LANES))
    acc_ref[...] = acc
    c_ref[...] = c

    @pl.when(ci == nc - 1)
    def _fin():
        ys = [jnp.sum(jnp.where(head_mask, acc[t * HEADS:(t + 1) * HEADS, :], 0.0), axis=0, keepdims=True)
              for t in range(nt)]
        o_ref[0] = _rms(jnp.concatenate(ys, axis=0), g_ref[...])


def _sb_sample(page_table, b_sb, q_s, k_own, v_own, g_sb, cache_kt, cache_vt):
    nseq, npages = page_table.shape
    nt = q_s.shape[1]
    page = cache_kt.shape[2]
    npg = PAGES_PER_STEP
    nc = npages // npg
    rows = nt * HEADS

    def page_spec(i):
        return pl.BlockSpec(
            (1, SB_W, page),
            lambda s, c, pt, i=i: (pt[s * npages + (npages - 1) - (c * npg + i)], 0, 0))

    seq_spec = pl.BlockSpec((1, nt, SB_W), lambda s, c, pt: (s, 0, 0))
    grid_spec = pltpu.PrefetchScalarGridSpec(
        num_scalar_prefetch=1,
        grid=(nseq, nc),
        in_specs=[pl.BlockSpec(memory_space=pltpu.SMEM), seq_spec, seq_spec, seq_spec,
                  pl.BlockSpec((1, SB_W), lambda s, c, pt: (0, 0))]
                 + [page_spec(i) for i in range(npg)] + [page_spec(i) for i in range(npg)],
        out_specs=pl.BlockSpec((1, nt, SB_W), lambda s, c, pt: (s, 0, 0)),
        scratch_shapes=[pltpu.VMEM((2 * rows, SB_W), BF16), pltpu.VMEM((rows, SB_W), F32),
                        pltpu.VMEM((rows, LANES), F32)],
    )
    return pl.pallas_call(
        functools.partial(_sb_sample_kernel, npg=npg, nt=nt, page=page),
        grid_spec=grid_spec,
        out_shape=jax.ShapeDtypeStruct((nseq, nt, SB_W), F32),
        compiler_params=pltpu.CompilerParams(
            dimension_semantics=("arbitrary", "arbitrary"), vmem_limit_bytes=VMEM_LIMIT),
        name="sb_sample",
    )(page_table.reshape(-1), b_sb, q_s, k_own, v_own, g_sb, *([cache_kt] * npg), *([cache_vt] * npg))


def _outproj_router_kernel(x_ref, yl_ref, ys_ref, wout_ref, gffn_ref, wr_ref, br_ref,
                           x1_ref, h8_ref, e_ref, w_ref, *, tm, precise):
    lru_rows = slice(0, LRU_W)
    sb_rows = slice(LRU_W, LRU_W + SB_W)
    if precise:
        x1 = (x_ref[...] + _dot3(yl_ref[...], wout_ref[lru_rows, :]) + _dot3(ys_ref[...], wout_ref[sb_rows, :]))
    else:
        x1 = (x_ref[...]
              + jnp.dot(yl_ref[...], wout_ref[lru_rows, :], preferred_element_type=F32)
              + jnp.dot(ys_ref[...], wout_ref[sb_rows, :], preferred_element_type=F32))
    x1_ref[...] = x1
    h = _rms(x1, gffn_ref[...])
    for j in range(ROW_CHUNKS):
        h8_ref[pl.ds(j, tm, stride=ROW_CHUNKS), :] = h[:, j * LANES:(j + 1) * LANES]

    logits = _dot3(h, wr_ref[...]) + br_ref[...]

    lane = lax.broadcasted_iota(I32, (tm, ROUTER_COLS), 1)
    big = jnp.asarray(ROUTER_COLS, I32)
    neg = jnp.asarray(-jnp.inf, F32)
    gl = jnp.where(lane < N_GROUPS, logits, neg)
    gmax = jnp.max(gl, axis=-1, keepdims=True)
    g_idx = jnp.min(jnp.where(gl == gmax, lane, big), axis=-1, keepdims=True)
    p_g = 1.0 / jnp.sum(jnp.exp(gl - gmax), axis=-1, keepdims=True)
    e_lo = N_GROUPS + g_idx * EXPERTS_PER_GROUP
    el = jnp.where((lane >= e_lo) & (lane < e_lo + EXPERTS_PER_GROUP), logits, neg)
    m1 = jnp.max(el, axis=-1, keepdims=True)
    i1 = jnp.min(jnp.where(el == m1, lane, big), axis=-1, keepdims=True)
    el2 = jnp.where(lane == i1, neg, el)
    m2 = jnp.max(el2, axis=-1, keepdims=True)
    i2 = jnp.min(jnp.where(el2 == m2, lane, big), axis=-1, keepdims=True)
    r = jnp.exp(m2 - m1)
    w1 = p_g / (1.0 + r)
    w2 = w1 * r
    e_ref[...] = jnp.where(lane == 0, i1 - N_GROUPS, jnp.where(lane == 1, i2 - N_GROUPS, 0))
    w_ref[...] = jnp.where(lane == 0, w1, jnp.where(lane == 1, w2, 0.0))


def _outproj_router(x2, ylru, ysb, w_out, g_ffn, w_r, b_r, tm):
    n = x2.shape[0]
    precise = ylru.dtype == F32
    assert ysb.dtype == ylru.dtype == w_out.dtype
    row_spec = lambda w: pl.BlockSpec((tm, w), lambda t: (t, 0))
    full = lambda a: pl.BlockSpec(a.shape, lambda t: (0,) * a.ndim)
    return pl.pallas_call(
        functools.partial(_outproj_router_kernel, tm=tm, precise=precise),
        grid=(n // tm,),
        in_specs=[row_spec(D_MODEL), row_spec(LRU_W), row_spec(SB_W), full(w_out), full(g_ffn), full(w_r),
                  full(b_r)],
        out_specs=(row_spec(D_MODEL), pl.BlockSpec((tm * ROW_CHUNKS, LANES), lambda t: (t, 0)),
                   row_spec(ROUTER_COLS), row_spec(ROUTER_COLS)),
        out_shape=(jax.ShapeDtypeStruct((n, D_MODEL), F32),
                   jax.ShapeDtypeStruct((n * ROW_CHUNKS, LANES), F32),
                   jax.ShapeDtypeStruct((n, ROUTER_COLS), I32),
                   jax.ShapeDtypeStruct((n, ROUTER_COLS), F32)),
        compiler_params=pltpu.CompilerParams(
            dimension_semantics=("arbitrary",), vmem_limit_bytes=VMEM_LIMIT),
        name="outproj_router",
    )(x2, ylru, ysb, w_out, g_ffn, w_r, b_r)


def _row_tile(ref, row):
    return ref.at[pl.ds(pl.multiple_of(row * ROW_CHUNKS, ROW_CHUNKS), ROW_CHUNKS)]


def _dispatch_kernel(pos_ref, h8_ref, xs_in_hbm, xs_hbm, sem, *, td):
    del xs_in_hbm

    def copy(r, k):
        return pltpu.make_async_copy(_row_tile(h8_ref, r), _row_tile(xs_hbm, pos_ref[0, 0, 2 * r + k]), sem)

    def start(r, c):
        copy(r, 0).start()
        copy(r, 1).start()
        return c

    def wait(r, c):
        copy(r, 0).wait()
        copy(r, 1).wait()
        return c

    lax.fori_loop(0, td, start, 0)
    lax.fori_loop(0, td, wait, 0)


def _dispatch(pos, h8, xs8, td):
    n = pos.shape[0]
    pos3 = pos.reshape(n // td, 1, 2 * td)
    return pl.pallas_call(
        functools.partial(_dispatch_kernel, td=td),
        grid=(n // td,),
        in_specs=[pl.BlockSpec((1, 1, 2 * td), lambda t: (t, 0, 0), memory_space=pltpu.SMEM),
                  pl.BlockSpec((td * ROW_CHUNKS, LANES), lambda t: (t, 0)),
                  pl.BlockSpec(memory_space=pl.ANY)],
        out_specs=pl.BlockSpec(memory_space=pl.ANY),
        out_shape=jax.ShapeDtypeStruct(xs8.shape, xs8.dtype),
        scratch_shapes=[pltpu.SemaphoreType.DMA(())],
        input_output_aliases={2: 0},
        compiler_params=pltpu.CompilerParams(dimension_semantics=("arbitrary",)),
        name="dispatch",
    )(pos3, h8, xs8)


def _experts_kernel(te_ref, tv_ref, tf_ref, xs_ref, wg_ref, wu_ref, wd_ref, ys_ref,
                    x2_ref, wgb_ref, wub_ref, wdb_ref, *, tm):
    del te_ref
    t = pl.program_id(0)

    @pl.when(tf_ref[t] == 1)
    def _cast_weights():
        wgb_ref[...] = wg_ref[0].astype(BF16)
        wub_ref[...] = wu_ref[0].astype(BF16)
        wdb_ref[...] = wd_ref[0].astype(BF16)

    @pl.when(tv_ref[t] == 1)
    def _compute():
        for j in range(ROW_CHUNKS):
            x2_ref[:, j * LANES:(j + 1) * LANES] = xs_ref[pl.ds(j, tm, stride=ROW_CHUNKS), :].astype(BF16)
        x = x2_ref[...]
        g = jnp.dot(x, wgb_ref[...], preferred_element_type=F32)
        u = jnp.dot(x, wub_ref[...], preferred_element_type=F32)
        act = (g * jax.nn.sigmoid(g) * u).astype(BF16)
        y = jnp.dot(act, wdb_ref[...], preferred_element_type=F32)
        for j in range(ROW_CHUNKS):
            ys_ref[pl.ds(j, tm, stride=ROW_CHUNKS), :] = y[:, j * LANES:(j + 1) * LANES]

    @pl.when(tv_ref[t] == 0)
    def _pad():
        ys_ref[...] = jnp.zeros_like(ys_ref)


def _experts(tile_e, tile_valid, tile_first, xs8, w_eg, w_eu, w_ed, tm):
    n_tiles = tile_e.shape[0]
    grid_spec = pltpu.PrefetchScalarGridSpec(
        num_scalar_prefetch=3,
        grid=(n_tiles,),
        in_specs=[
            pl.BlockSpec((tm * ROW_CHUNKS, LANES), lambda t, te, tv, tf: (t, 0)),
            pl.BlockSpec((1, D_MODEL, D_EXPERT), lambda t, te, tv, tf: (te[t], 0, 0)),
            pl.BlockSpec((1, D_MODEL, D_EXPERT), lambda t, te, tv, tf: (te[t], 0, 0)),
            pl.BlockSpec((1, D_EXPERT, D_MODEL), lambda t, te, tv, tf: (te[t], 0, 0)),
        ],
        out_specs=pl.BlockSpec((tm * ROW_CHUNKS, LANES), lambda t, te, tv, tf: (t, 0)),
        scratch_shapes=[pltpu.VMEM((tm, D_MODEL), BF16), pltpu.VMEM((D_MODEL, D_EXPERT), BF16),
                        pltpu.VMEM((D_MODEL, D_EXPERT), BF16), pltpu.VMEM((D_EXPERT, D_MODEL), BF16)],
    )
    return pl.pallas_call(
        functools.partial(_experts_kernel, tm=tm),
        grid_spec=grid_spec,
        out_shape=jax.ShapeDtypeStruct(xs8.shape, F32),
        compiler_params=pltpu.CompilerParams(
            dimension_semantics=("arbitrary",), vmem_limit_bytes=VMEM_LIMIT),
        name="experts",
    )(tile_e, tile_valid, tile_first, xs8, w_eg, w_eu, w_ed)


def _combine_kernel(pos_ref, x1_ref, wsel_ref, gfin_ref, ys_hbm, o_ref, buf_ref, sem, *, td):
    def copy(r, k):
        return pltpu.make_async_copy(_row_tile(ys_hbm, pos_ref[0, 0, 2 * r + k]), _row_tile(buf_ref.at[k], r), sem)

    def start(r, c):
        copy(r, 0).start()
        copy(r, 1).start()
        return c

    def wait(r, c):
        copy(r, 0).wait()
        copy(r, 1).wait()
        return c

    lax.fori_loop(0, td, start, 0)
    lax.fori_loop(0, td, wait, 0)

    w = wsel_ref[...]
    w0 = w[:, 0:1]
    w1 = w[:, 1:2]
    cols = []
    for j in range(ROW_CHUNKS):
        cols.append(x1_ref[:, j * LANES:(j + 1) * LANES]
                    + w0 * buf_ref[0, pl.ds(j, td, stride=ROW_CHUNKS), :]
                    + w1 * buf_ref[1, pl.ds(j, td, stride=ROW_CHUNKS), :])
    o_ref[...] = _rms(jnp.concatenate(cols, axis=1), gfin_ref[...])


def _combine_final(pos, x1, wsel, g_final, ys8, td):
    n = pos.shape[0]
    pos3 = pos.reshape(n // td, 1, 2 * td)
    return pl.pallas_call(
        functools.partial(_combine_kernel, td=td),
        grid=(n // td,),
        in_specs=[pl.BlockSpec((1, 1, 2 * td), lambda t: (t, 0, 0), memory_space=pltpu.SMEM),
                  pl.BlockSpec((td, D_MODEL), lambda t: (t, 0)),
                  pl.BlockSpec((td, ROUTER_COLS), lambda t: (t, 0)),
                  pl.BlockSpec((1, D_MODEL), lambda t: (0, 0)),
                  pl.BlockSpec(memory_space=pl.ANY)],
        out_specs=pl.BlockSpec((td, D_MODEL), lambda t: (t, 0)),
        out_shape=jax.ShapeDtypeStruct((n, D_MODEL), F32),
        scratch_shapes=[pltpu.VMEM((2, td * ROW_CHUNKS, LANES), F32), pltpu.SemaphoreType.DMA(())],
        compiler_params=pltpu.CompilerParams(
            dimension_semantics=("arbitrary",), vmem_limit_bytes=VMEM_LIMIT),
        name="combine_final",
    )(pos3, x1, wsel, g_final, ys8)


def _routing_plan(e_sel, tm, n_tiles):
    e_flat = e_sel.reshape(-1)
    onehot = (e_flat[:, None] == jnp.arange(N_EXPERTS, dtype=I32)[None, :]).astype(I32)
    csum = jnp.cumsum(onehot, axis=0)
    rank = jnp.sum(onehot * csum, axis=1) - 1
    counts = csum[-1]
    tiles_per = (counts + tm - 1) // tm
    tile_end = jnp.cumsum(tiles_per)
    tile_start = tile_end - tiles_per
    pos = (tile_start * tm)[e_flat] + rank
    total = tile_end[-1]
    t = jnp.arange(n_tiles, dtype=I32)
    tile_e = jnp.minimum(jnp.sum((t[:, None] >= tile_end[None, :]).astype(I32), axis=1), N_EXPERTS - 1)
    last_e = jnp.max(jnp.where(counts > 0, jnp.arange(N_EXPERTS, dtype=I32), 0))
    valid = t < total
    tile_e = jnp.where(valid, tile_e, last_e).astype(I32)
    first = (valid & (t == tile_start[tile_e])).astype(I32)
    return pos.reshape(-1, 2).astype(I32), tile_e, valid.astype(I32), first


def _block_diag(w):
    nb, d, _ = w.shape
    eye = jnp.eye(nb, dtype=w.dtype)
    return (eye[:, None, :, None] * w[:, :, None, :]).reshape(nb * d, nb * d)


def kernel(x_prompt, x_sample, cache_k, cache_v, state_lru_h, state_conv, page_table, g_mix, w_in, conv_w, conv_b, w_a, b_a, w_i, b_i, lam, b_sb, g_out_lru, g_out_sb, w_out, g_ffn, w_rg, b_rg, w_re, b_re, w_eg, w_eu, w_ed, g_final):
    depth = w_in.shape[0]
    assert depth == 1, "single-layer trunk"
    batch, seq, _ = x_prompt.shape
    nseq, nt, _ = x_sample.shape
    npool, page = cache_k.shape[1], cache_k.shape[2]
    n_p = batch * seq
    n_s = nseq * nt
    row = lambda a: a.reshape(1, -1)

    w_gates = jnp.concatenate([_block_diag(w_a[0]), _block_diag(w_i[0])], axis=1)
    b_gates = jnp.concatenate([b_a[0], b_i[0]]).reshape(1, -1)
    w_r = jnp.zeros((D_MODEL, ROUTER_COLS), F32)
    w_r = w_r.at[:, :N_GROUPS].set(w_rg[0]).at[:, N_GROUPS:N_GROUPS + N_EXPERTS].set(w_re[0])
    b_r = jnp.zeros((1, ROUTER_COLS), F32)
    b_r = b_r.at[0, :N_GROUPS].set(b_rg[0]).at[0, N_GROUPS:N_GROUPS + N_EXPERTS].set(b_re[0])
    lru_tail = (row(lam[0]), row(g_out_lru[0]))

    x_p = x_prompt.reshape(n_p, D_MODEL)
    q_p, k_p, v_p, kb_p, vt_p, ylru_p, ht_p, cbuf_p = _inproj_lru_prompt(
        x_p, row(g_mix[0]), w_in[0].astype(BF16), conv_w[0], row(conv_b[0]), w_gates.astype(BF16), b_gates,
        *lru_tail, batch, seq)
    ysb_p = _sb_prompt(b_sb[0], q_p, kb_p, vt_p, row(g_out_sb[0]), batch, seq)

    x_s = x_sample.transpose(1, 0, 2).reshape(n_s, D_MODEL)
    buf_tm = state_conv[0].transpose(1, 0, 2).reshape((CONV_W - 1) * nseq, LRU_W)
    q_s, k_s, v_s, ylru_s, ht_s, cbuf_s = _inproj_lru_sample(
        x_s, state_lru_h[0], buf_tm, row(g_mix[0]), w_in[0], conv_w[0], row(conv_b[0]), w_gates, b_gates,
        *lru_tail, nseq, nt)
    seq_major = lambda a, w: a.reshape(-1, nseq, w).transpose(1, 0, 2)
    q_sm, k_sm, v_sm = seq_major(q_s, SB_W), seq_major(k_s, SB_W), seq_major(v_s, SB_W)
    pages_t = lambda c: c[0].transpose(0, 2, 3, 1).reshape(npool, SB_W, page)
    ysb_s = _sb_sample(page_table, b_sb[0], q_sm, k_sm, v_sm, row(g_out_sb[0]),
                       pages_t(cache_k), pages_t(cache_v))
    x_s2 = x_sample.reshape(n_s, D_MODEL)
    ylru_s2 = seq_major(ylru_s, LRU_W).reshape(n_s, LRU_W)
    ysb_s2 = ysb_s.reshape(n_s, SB_W)

    router_args = (row(g_ffn[0]), w_r, b_r)
    x1_p, h8_p, e_p, wsel_p = _outproj_router(x_p, ylru_p, ysb_p, w_out[0].astype(BF16), *router_args, TM_OUT)
    x1_s, h8_s, e_s, wsel_s = _outproj_router(x_s2, ylru_s2, ysb_s2, w_out[0], *router_args, n_s)

    n_all = n_p + n_s
    n_tiles = -(-2 * n_all // TM_EXP) + N_EXPERTS
    e_all = jnp.concatenate([e_p[:, :2], e_s[:, :2]], axis=0)
    pos, tile_e, tile_valid, tile_first = _routing_plan(e_all, TM_EXP, n_tiles)
    pos_p, pos_s = pos[:n_p], pos[n_p:]

    xs8 = jnp.zeros((n_tiles * TM_EXP * ROW_CHUNKS, LANES), F32)
    xs8 = _dispatch(pos_p, h8_p, xs8, TD)
    xs8 = _dispatch(pos_s, h8_s, xs8, n_s)
    ys8 = _experts(tile_e, tile_valid, tile_first, xs8, w_eg[0], w_eu[0], w_ed[0], TM_EXP)

    gfin = row(g_final)
    y_p = _combine_final(pos_p, x1_p, wsel_p, gfin, ys8, TD)
    y_s = _combine_final(pos_s, x1_s, wsel_s, gfin, ys8, n_s)

    heads = lambda a, lead: a.reshape(*lead, HEADS, HEAD_DIM)
    return (
        y_p.reshape(batch, seq, D_MODEL),
        y_s.reshape(nseq, nt, D_MODEL),
        heads(k_p, (1, batch, seq)),
        heads(v_p, (1, batch, seq)),
        ht_p.reshape(1, batch, LRU_W),
        cbuf_p.reshape(1, batch, CONV_W - 1, LRU_W),
        heads(k_sm, (1, nseq, nt)),
        heads(v_sm, (1, nseq, nt)),
        ht_s.reshape(1, nseq, LRU_W),
        cbuf_s.reshape(CONV_W - 1, nseq, LRU_W).transpose(1, 0, 2)[None],
    )
```

```python
import functools
import math

import jax
import jax.numpy as jnp
from jax import lax
from jax.experimental import pallas as pl
from jax.experimental.pallas import tpu as pltpu

F32 = jnp.float32
BF16 = jnp.bfloat16
I32 = jnp.int32

D_MODEL = 1024
LRU_W = 512
SB_W = 512
HEADS = 8
HEAD_DIM = 64
CONV_W = 4
LRU_C = 8.0
EPS = 1e-6
N_GROUPS = 4
EXPERTS_PER_GROUP = 8
N_EXPERTS = 32
D_EXPERT = 512

LANES = 128
SUBLANES = 8
ROW_CHUNKS = D_MODEL // LANES

TM_PROJ = 512
TQ = 256
TM_OUT = 512
TM_EXP = 256
TD = 256
PAGES_PER_STEP = 16
ROUTER_COLS = LANES
VMEM_LIMIT = 56 * 1024 * 1024


def _softplus(z):
    return jnp.maximum(z, 0.0) + jnp.log1p(jnp.exp(-jnp.abs(z)))


def _log_beta_terms(z, accurate):
    t = jnp.exp(-jnp.abs(z))
    log_beta = jnp.minimum(z, 0.0) - (jnp.log1p(t) if accurate else jnp.log(1.0 + t))
    return log_beta - z, log_beta


def _rms(x, g):
    return x * lax.rsqrt(jnp.mean(x * x, axis=-1, keepdims=True) + EPS) * g


def _gelu_tanh(x):
    c = math.sqrt(2.0 / math.pi)
    return x * (0.5 * (1.0 + jnp.tanh(c * (x + 0.044715 * (x * x * x)))))


def _neg_expm1(x):
    u = jnp.exp(x)
    near = x > -1.0
    ratio = x / jnp.where(near & (u < 1.0), jnp.log(u), -1.0)
    return jnp.where(near, jnp.where(u < 1.0, (1.0 - u) * ratio, -x), 1.0 - u)


def _lru_gates(u, gates, lam):
    r = jax.nn.sigmoid(gates[:, :LRU_W])
    i = jax.nn.sigmoid(gates[:, LRU_W:])
    log_a = (-LRU_C) * r * _softplus(-lam)
    a = jnp.exp(log_a)
    xin = jnp.sqrt(_neg_expm1(2.0 * log_a)) * (i * u)
    return a, xin


def _split_bf16(x):
    hi = x.astype(BF16)
    lo = (x - hi.astype(F32)).astype(BF16)
    return hi, lo


def _dot3(x, w):
    x_hi, x_lo = _split_bf16(x)
    w_hi, w_lo = _split_bf16(w)
    n = x.shape[0]
    top = jnp.dot(jnp.concatenate([x_hi, x_lo], axis=0), w_hi, preferred_element_type=F32)
    return top[:n] + top[n:] + jnp.dot(x_hi, w_lo, preferred_element_type=F32)


def _inproj_lru_kernel(x_ref, gmix_ref, win_ref, cw_ref, cb_ref, wg_ref, bg_ref, lam_ref, glru_ref,
                       q_ref, kt_ref, vt_ref, kb_ref, vtb_ref, ylru_ref, ht_ref, cbuf_ref,
                       xlp_ref, a_ref, s_ref, h_ref, *, tm, tk):
    t = pl.program_id(1)
    nt = pl.num_programs(1)

    @pl.when(t == 0)
    def _init():
        xlp_ref[0:SUBLANES, :] = jnp.zeros((SUBLANES, LRU_W), F32)
        h_ref[...] = jnp.zeros_like(h_ref)

    hn = _rms(x_ref[...], gmix_ref[...]).astype(BF16)

    def proj(c):
        return jnp.dot(hn, win_ref[:, c * LRU_W:(c + 1) * LRU_W], preferred_element_type=F32)

    xl = proj(0)
    gl = proj(1)
    q_ref[...] = proj(2).astype(BF16)
    k = proj(3)
    kt_ref[0] = k.T
    kb_ref[...] = k.astype(BF16)
    v_t = proj(4).T
    vt_ref[0] = v_t
    for c in range(tm // tk):
        vtb_ref[0, c] = v_t[:, c * tk:(c + 1) * tk].astype(BF16)

    xlp_ref[SUBLANES:SUBLANES + tm, :] = xl
    cw = cw_ref[...]
    u = cb_ref[...] + cw[3:4, :] * xl
    for j in range(1, CONV_W):
        u = u + cw[3 - j:4 - j, :] * xlp_ref[pl.ds(SUBLANES - j, tm), :]
    xlp_ref[0:SUBLANES, :] = xl[tm - SUBLANES:tm, :]

    gates = jnp.dot(u.astype(BF16), wg_ref[...], preferred_element_type=F32) + bg_ref[...]
    a, xin = _lru_gates(u, gates, lam_ref[...])
    a_ref[...] = a
    s_ref[...] = xin

    row = lax.broadcasted_iota(I32, (SUBLANES, LRU_W), 0)

    def group(g, h):
        r0 = pl.multiple_of(g * SUBLANES, SUBLANES)
        av = a_ref[pl.ds(r0, SUBLANES), :]
        xv = s_ref[pl.ds(r0, SUBLANES), :]
        for d in (1, 2, 4):
            valid = row >= d
            xv = jnp.where(valid, xv + av * pltpu.roll(xv, d, 0), xv)
            av = jnp.where(valid, av * pltpu.roll(av, d, 0), av)
        hv = xv + av * h
        s_ref[pl.ds(r0, SUBLANES), :] = hv
        return hv[SUBLANES - 1:SUBLANES, :]

    h = lax.fori_loop(0, tm // SUBLANES, group, h_ref[0:1, :])
    h_ref[0:1, :] = h

    y = s_ref[...] * _gelu_tanh(gl)
    ylru_ref[...] = _rms(y, glru_ref[...]).astype(BF16)

    @pl.when(t == nt - 1)
    def _fin():
        ht_ref[0] = h
        cbuf_ref[0] = xl[tm - (CONV_W - 1):tm, :]


def _inproj_lru_prompt(x2, g_mix, w_in_bf, conv_w, conv_b, w_gates, b_gates, lam, g_lru, batch, seq):
    tm, tk = TM_PROJ, TQ
    nt = seq // tm
    n = batch * seq
    row_spec = lambda w: pl.BlockSpec((tm, w), lambda b, t: (b * nt + t, 0))
    full = lambda a: pl.BlockSpec(a.shape, lambda b, t: (0,) * a.ndim)
    t_spec = pl.BlockSpec((1, SB_W, tm), lambda b, t: (b, 0, t))
    out_shape = (
        jax.ShapeDtypeStruct((n, SB_W), BF16),
        jax.ShapeDtypeStruct((batch, SB_W, seq), F32),
        jax.ShapeDtypeStruct((batch, SB_W, seq), F32),
        jax.ShapeDtypeStruct((n, SB_W), BF16),
        jax.ShapeDtypeStruct((batch, seq // tk, SB_W, tk), BF16),
        jax.ShapeDtypeStruct((n, LRU_W), BF16),
        jax.ShapeDtypeStruct((batch, 1, LRU_W), F32),
        jax.ShapeDtypeStruct((batch, CONV_W - 1, LRU_W), F32),
    )
    out_specs = (
        row_spec(SB_W), t_spec, t_spec, row_spec(SB_W),
        pl.BlockSpec((1, tm // tk, SB_W, tk), lambda b, t: (b, t, 0, 0)),
        row_spec(LRU_W),
        pl.BlockSpec((1, 1, LRU_W), lambda b, t: (b, 0, 0)),
        pl.BlockSpec((1, CONV_W - 1, LRU_W), lambda b, t: (b, 0, 0)),
    )
    args = (x2, g_mix, w_in_bf, conv_w, conv_b, w_gates, b_gates, lam, g_lru)
    return pl.pallas_call(
        functools.partial(_inproj_lru_kernel, tm=tm, tk=tk),
        grid=(batch, nt),
        in_specs=[row_spec(D_MODEL)] + [full(a) for a in args[1:]],
        out_specs=out_specs,
        out_shape=out_shape,
        scratch_shapes=[
            pltpu.VMEM((tm + SUBLANES, LRU_W), F32),
            pltpu.VMEM((tm, LRU_W), F32),
            pltpu.VMEM((tm, LRU_W), F32),
            pltpu.VMEM((SUBLANES, LRU_W), F32),
        ],
        compiler_params=pltpu.CompilerParams(
            dimension_semantics=("arbitrary", "arbitrary"), vmem_limit_bytes=VMEM_LIMIT),
        name="inproj_lru_prompt",
    )(*args)


def _inproj_lru_sample_kernel(x_ref, h0_ref, buf_ref, gmix_ref, win_ref, cw_ref, cb_ref, wg_ref, bg_ref,
                              lam_ref, glru_ref, q_ref, k_ref, v_ref, ylru_ref, ht_ref, cbuf_ref,
                              *, nseq, nt):
    hn = _rms(x_ref[...], gmix_ref[...])

    def proj(c):
        return _dot3(hn, win_ref[:, c * LRU_W:(c + 1) * LRU_W])

    xl = proj(0)
    gl = proj(1)
    q_ref[...] = proj(2)
    k_ref[...] = proj(3)
    v_ref[...] = proj(4)

    hist = [buf_ref[m * nseq:(m + 1) * nseq, :] for m in range(CONV_W - 1)]
    hist += [xl[t * nseq:(t + 1) * nseq, :] for t in range(nt)]
    cw = cw_ref[...]
    us = []
    for t in range(nt):
        ut = cb_ref[...] + cw[0:1, :] * hist[t]
        for j in range(1, CONV_W):
            ut = ut + cw[j:j + 1, :] * hist[t + j]
        us.append(ut)
    u = jnp.concatenate(us, axis=0)
    gates = _dot3(u, wg_ref[...]) + bg_ref[...]
    a, xin = _lru_gates(u, gates, lam_ref[...])
    h = h0_ref[...]
    hs = []
    for t in range(nt):
        h = a[t * nseq:(t + 1) * nseq, :] * h + xin[t * nseq:(t + 1) * nseq, :]
        hs.append(h)
    y = jnp.concatenate(hs, axis=0) * _gelu_tanh(gl)
    ylru_ref[...] = _rms(y, glru_ref[...])
    ht_ref[...] = h
    cbuf_ref[...] = jnp.concatenate(hist[nt:nt + CONV_W - 1], axis=0)


def _inproj_lru_sample(x_tm, h0, buf_tm, g_mix, w_in, conv_w, conv_b, w_gates, b_gates, lam, g_lru,
                       nseq, nt):
    n = nseq * nt
    out_shape = (
        jax.ShapeDtypeStruct((n, SB_W), F32),
        jax.ShapeDtypeStruct((n, SB_W), F32),
        jax.ShapeDtypeStruct((n, SB_W), F32),
        jax.ShapeDtypeStruct((n, LRU_W), F32),
        jax.ShapeDtypeStruct((nseq, LRU_W), F32),
        jax.ShapeDtypeStruct(((CONV_W - 1) * nseq, LRU_W), F32),
    )
    return pl.pallas_call(
        functools.partial(_inproj_lru_sample_kernel, nseq=nseq, nt=nt),
        out_shape=out_shape,
        compiler_params=pltpu.CompilerParams(vmem_limit_bytes=VMEM_LIMIT),
        name="inproj_lru_sample",
    )(x_tm, h0, buf_tm, g_mix, w_in, conv_w, conv_b, w_gates, b_gates, lam, g_lru)


def _sb_prompt_kernel(bsb_ref, q_ref, k_ref, vt_ref, g_ref, o_ref, acc_ref, c_ref, qm_ref, *, tq):
    i = pl.program_id(1)
    tk = tq
    rowk = lax.broadcasted_iota(I32, (tk, tq), 0)
    colq = lax.broadcasted_iota(I32, (tk, tq), 1)
    diag_mask = rowk < colq
    later_mat = (lax.broadcasted_iota(I32, (tk, tk), 1) > lax.broadcasted_iota(I32, (tk, tk), 0)).astype(BF16)
    lane = lax.broadcasted_iota(I32, (tq, LANES), 1)
    scale = jnp.asarray(HEAD_DIM ** -0.5, BF16)
    heads_per_tile = LANES // HEAD_DIM

    for h in range(HEADS):
        pair, half = divmod(h, heads_per_tile)
        in_head = (lane >= half * HEAD_DIM) & (lane < (half + 1) * HEAD_DIM)
        qm_ref[h] = jnp.where(in_head, q_ref[:, pair * LANES:(pair + 1) * LANES] * scale, jnp.zeros((), BF16))
    acc_ref[...] = jnp.zeros_like(acc_ref)
    c_ref[...] = jnp.zeros_like(c_ref)

    def block(j, masked):
        k0 = pl.multiple_of(j * tk, tk)
        kbs = [k_ref[pl.ds(k0, tk), p * LANES:(p + 1) * LANES] for p in range(HEADS // heads_per_tile)]
        ss = [lax.dot_general(kbs[h // heads_per_tile], qm_ref[h], (((1,), (1,)), ((), ())),
                              preferred_element_type=F32) + bsb_ref[h] for h in range(HEADS)]
        l1s, lbs = [], []
        for h in range(HEADS):
            l1, lb = _log_beta_terms(ss[h], accurate=False)
            if masked:
                l1 = jnp.where(diag_mask, l1, 0.0)
            l1s.append(l1.astype(BF16))
            lbs.append(lb)
        exs = [jnp.dot(later_mat, l1s[h], preferred_element_type=F32) for h in range(HEADS)]
        ws = []
        for h in range(HEADS):
            w = jnp.exp(lbs[h] + exs[h] + c_ref[h:h + 1, :])
            if masked:
                w = jnp.where(diag_mask, w, 0.0)
            ws.append(w.astype(BF16))
        for h in range(HEADS):
            rows = slice(h * HEAD_DIM, (h + 1) * HEAD_DIM)
            acc_ref[rows, :] += jnp.dot(vt_ref[0, j, rows, :], ws[h], preferred_element_type=F32)
            c_ref[h:h + 1, :] += exs[h][0:1, :] + l1s[h][0:1, :]

    block(i, True)

    def body(jj, carry):
        block(i - 1 - jj, False)
        return carry

    lax.fori_loop(0, i, body, 0)
    o_ref[...] = _rms(acc_ref[...].T, g_ref[...]).astype(BF16)


def _sb_prompt(b_sb, q_bf, k_bf, vt_bf, g_sb, batch, seq):
    tq = TQ
    nq = seq // tq
    n = batch * seq
    return pl.pallas_call(
        functools.partial(_sb_prompt_kernel, tq=tq),
        grid=(batch, nq),
        in_specs=[
            pl.BlockSpec(memory_space=pltpu.SMEM),
            pl.BlockSpec((tq, SB_W), lambda b, i: (b * nq + i, 0)),
            pl.BlockSpec((seq, SB_W), lambda b, i: (b, 0)),
            pl.BlockSpec((1, nq, SB_W, tq), lambda b, i: (b, 0, 0, 0)),
            pl.BlockSpec((1, SB_W), lambda b, i: (0, 0)),
        ],
        out_specs=pl.BlockSpec((tq, SB_W), lambda b, i: (b * nq + i, 0)),
        out_shape=jax.ShapeDtypeStruct((n, SB_W), BF16),
        scratch_shapes=[pltpu.VMEM((SB_W, tq), F32), pltpu.VMEM((HEADS, tq), F32),
                        pltpu.VMEM((HEADS, tq, LANES), BF16)],
        compiler_params=pltpu.CompilerParams(
            dimension_semantics=("arbitrary", "arbitrary"), vmem_limit_bytes=VMEM_LIMIT),
        name="sb_prompt",
    )(b_sb, q_bf, k_bf, vt_bf, g_sb)


def _sb_sample_kernel(pt_ref, bsb_ref, q_ref, kown_ref, vown_ref, g_ref, *rest, npg, nt, page):
    kp = rest[:npg]
    vp = rest[npg:2 * npg]
    o_ref = rest[2 * npg]
    qbd_ref, acc_ref, c_ref = rest[2 * npg + 1:]
    ci = pl.program_id(1)
    nc = pl.num_programs(1)
    rows = nt * HEADS
    rowi = lax.broadcasted_iota(I32, (rows, 1), 0)
    row_h = rowi & (HEADS - 1)
    row_t = rowi >> int(math.log2(HEADS))
    bias = jnp.zeros((rows, 1), F32)
    for h in range(HEADS):
        bias = jnp.where(row_h == h, bsb_ref[h], bias)
    head_of_lane = lax.broadcasted_iota(I32, (HEADS, SB_W), 1) // HEAD_DIM
    head_mask = head_of_lane == lax.broadcasted_iota(I32, (HEADS, SB_W), 0)

    @pl.when(ci == 0)
    def _init():
        q = q_ref[0] * (HEAD_DIM ** -0.5)
        qbd = jnp.concatenate(
            [jnp.where(head_mask, jnp.broadcast_to(q[t:t + 1, :], (HEADS, SB_W)), 0.0) for t in range(nt)],
            axis=0)
        qbd_ref[...] = jnp.concatenate(_split_bf16(qbd), axis=0)
        kown = kown_ref[0]
        vown = vown_ref[0]
        c = jnp.zeros((rows, 1), F32)
        acc = jnp.zeros((rows, SB_W), F32)
        for s in range(nt - 1, -1, -1):
            z = jnp.sum(qbd * kown[s:s + 1, :], axis=-1, keepdims=True) + bias
            vis = s < row_t
            l1, lb = _log_beta_terms(z, accurate=True)
            l1 = jnp.where(vis, l1, 0.0)
            w = jnp.where(vis, jnp.exp(lb + c), 0.0)
            acc = acc + w * vown[s:s + 1, :]
            c = c + l1
        acc_ref[...] = acc
        c_ref[...] = jnp.broadcast_to(c, (rows, LANES))

    later_mat = (lax.broadcasted_iota(I32, (page, page), 0) > lax.broadcasted_iota(I32, (page, page), 1)).astype(BF16)
    nt_dims = (((1,), (1,)), ((), ()))
    q2 = qbd_ref[...]
    q_hi = q2[:rows]

    def fold(x):
        return x[:rows] + x[rows:]

    ss = []
    for i in range(npg):
        kt_hi, kt_lo = _split_bf16(kp[i][0])
        ss.append(fold(jnp.dot(q2, kt_hi, preferred_element_type=F32))
                  + jnp.dot(q_hi, kt_lo, preferred_element_type=F32) + bias)
    l1s, lbs, exs = [], [], []
    for i in range(npg):
        l1, lb = _log_beta_terms(ss[i], accurate=True)
        l1s.append(l1)
        lbs.append(lb)
        exs.append(fold(jnp.dot(jnp.concatenate(_split_bf16(l1), axis=0), later_mat,
                                preferred_element_type=F32)))
    c = c_ref[...]
    acc = acc_ref[...]
    for i in range(npg):
        w = jnp.exp(lbs[i] + exs[i] + c)
        w_hi, w_lo = _split_bf16(w)
        vt_hi, vt_lo = _split_bf16(vp[i][0])
        acc = (acc + fold(lax.dot_general(jnp.concatenate([w_hi, w_lo], axis=0), vt_hi, nt_dims,
                                          preferred_element_type=F32))
               + lax.dot_general(w_hi, vt_lo, nt_dims, preferred_element_type=F32))
        c = c + jnp.broadcast_to(exs[i][:, 0:1] + l1s[i][:, 0:1], (rows, LANES))
    acc_ref[...] = acc
    c_ref[...] = c

    @pl.when(ci == nc - 1)
    def _fin():
        ys = [jnp.sum(jnp.where(head_mask, acc[t * HEADS:(t + 1) * HEADS, :], 0.0), axis=0, keepdims=True)
              for t in range(nt)]
        o_ref[0] = _rms(jnp.concatenate(ys, axis=0), g_ref[...])


def _sb_sample(page_table, b_sb, q_s, k_own, v_own, g_sb, cache_kt, cache_vt):
    nseq, npages = page_table.shape
    nt = q_s.shape[1]
    page = cache_kt.shape[2]
    npg = PAGES_PER_STEP
    nc = npages // npg
    rows = nt * HEADS

    def page_spec(i):
        return pl.BlockSpec(
            (1, SB_W, page),
            lambda s, c, pt, i=i: (pt[s * npages + (npages - 1) - (c * npg + i)], 0, 0))

    seq_spec = pl.BlockSpec((1, nt, SB_W), lambda s, c, pt: (s, 0, 0))
    grid_spec = pltpu.PrefetchScalarGridSpec(
        num_scalar_prefetch=1,
        grid=(nseq, nc),
        in_specs=[pl.BlockSpec(memory_space=pltpu.SMEM), seq_spec, seq_spec, seq_spec,
                  pl.BlockSpec((1, SB_W), lambda s, c, pt: (0, 0))]
                 + [page_spec(i) for i in range(npg)] + [page_spec(i) for i in range(npg)],
        out_specs=pl.BlockSpec((1, nt, SB_W), lambda s, c, pt: (s, 0, 0)),
        scratch_shapes=[pltpu.VMEM((2 * rows, SB_W), BF16), pltpu.VMEM((rows, SB_W), F32),
                        pltpu.VMEM((rows, LANES), F32)],
    )
    return pl.pallas_call(
        functools.partial(_sb_sample_kernel, npg=npg, nt=nt, page=page),
        grid_spec=grid_spec,
        out_shape=jax.ShapeDtypeStruct((nseq, nt, SB_W), F32),
        compiler_params=pltpu.CompilerParams(
            dimension_semantics=("arbitrary", "arbitrary"), vmem_limit_bytes=VMEM_LIMIT),
        name="sb_sample",
    )(page_table.reshape(-1), b_sb, q_s, k_own, v_own, g_sb, *([cache_kt] * npg), *([cache_vt] * npg))


def _outproj_router_kernel(x_ref, yl_ref, ys_ref, wout_ref, gffn_ref, wr_ref, br_ref,
                           x1_ref, h8_ref, e_ref, w_ref, *, tm, precise):
    lru_rows = slice(0, LRU_W)
    sb_rows = slice(LRU_W, LRU_W + SB_W)
    if precise:
        x1 = (x_ref[...] + _dot3(yl_ref[...], wout_ref[lru_rows, :]) + _dot3(ys_ref[...], wout_ref[sb_rows, :]))
    else:
        x1 = (x_ref[...]
              + jnp.dot(yl_ref[...], wout_ref[lru_rows, :], preferred_element_type=F32)
              + jnp.dot(ys_ref[...], wout_ref[sb_rows, :], preferred_element_type=F32))
    x1_ref[...] = x1
    h = _rms(x1, gffn_ref[...])
    for j in range(ROW_CHUNKS):
        h8_ref[pl.ds(j, tm, stride=ROW_CHUNKS), :] = h[:, j * LANES:(j + 1) * LANES]

    logits = _dot3(h, wr_ref[...]) + br_ref[...]

    lane = lax.broadcasted_iota(I32, (tm, ROUTER_COLS), 1)
    big = jnp.asarray(ROUTER_COLS, I32)
    neg = jnp.asarray(-jnp.inf, F32)
    gl = jnp.where(lane < N_GROUPS, logits, neg)
    gmax = jnp.max(gl, axis=-1, keepdims=True)
    g_idx = jnp.min(jnp.where(gl == gmax, lane, big), axis=-1, keepdims=True)
    p_g = 1.0 / jnp.sum(jnp.exp(gl - gmax), axis=-1, keepdims=True)
    e_lo = N_GROUPS + g_idx * EXPERTS_PER_GROUP
    el = jnp.where((lane >= e_lo) & (lane < e_lo + EXPERTS_PER_GROUP), logits, neg)
    m1 = jnp.max(el, axis=-1, keepdims=True)
    i1 = jnp.min(jnp.where(el == m1, lane, big), axis=-1, keepdims=True)
    el2 = jnp.where(lane == i1, neg, el)
    m2 = jnp.max(el2, axis=-1, keepdims=True)
    i2 = jnp.min(jnp.where(el2 == m2, lane, big), axis=-1, keepdims=True)
    r = jnp.exp(m2 - m1)
    w1 = p_g / (1.0 + r)
    w2 = w1 * r
    e_ref[...] = jnp.where(lane == 0, i1 - N_GROUPS, jnp.where(lane == 1, i2 - N_GROUPS, 0))
    w_ref[...] = jnp.where(lane == 0, w1, jnp.where(lane == 1, w2, 0.0))


def _outproj_router(x2, ylru, ysb, w_out, g_ffn, w_r, b_r, tm):
    n = x2.shape[0]
    precise = ylru.dtype == F32
    assert ysb.dtype == ylru.dtype == w_out.dtype
    row_spec = lambda w: pl.BlockSpec((tm, w), lambda t: (t, 0))
    full = lambda a: pl.BlockSpec(a.shape, lambda t: (0,) * a.ndim)
    return pl.pallas_call(
        functools.partial(_outproj_router_kernel, tm=tm, precise=precise),
        grid=(n // tm,),
        in_specs=[row_spec(D_MODEL), row_spec(LRU_W), row_spec(SB_W), full(w_out), full(g_ffn), full(w_r),
                  full(b_r)],
        out_specs=(row_spec(D_MODEL), pl.BlockSpec((tm * ROW_CHUNKS, LANES), lambda t: (t, 0)),
                   row_spec(ROUTER_COLS), row_spec(ROUTER_COLS)),
        out_shape=(jax.ShapeDtypeStruct((n, D_MODEL), F32),
                   jax.ShapeDtypeStruct((n * ROW_CHUNKS, LANES), F32),
                   jax.ShapeDtypeStruct((n, ROUTER_COLS), I32),
                   jax.ShapeDtypeStruct((n, ROUTER_COLS), F32)),
        compiler_params=pltpu.CompilerParams(
            dimension_semantics=("arbitrary",), vmem_limit_bytes=VMEM_LIMIT),
        name="outproj_router",
    )(x2, ylru, ysb, w_out, g_ffn, w_r, b_r)


def _row_tile(ref, row):
    return ref.at[pl.ds(pl.multiple_of(row * ROW_CHUNKS, ROW_CHUNKS), ROW_CHUNKS)]


def _dispatch_kernel(pos_ref, h8_ref, xs_in_hbm, xs_hbm, sem, *, td):
    del xs_in_hbm

    def copy(r, k):
        return pltpu.make_async_copy(_row_tile(h8_ref, r), _row_tile(xs_hbm, pos_ref[0, 0, 2 * r + k]), sem)

    def start(r, c):
        copy(r, 0).start(priority=0)
        copy(r, 1).start(priority=1)
        return c

    def wait(r, c):
        copy(r, 0).wait()
        copy(r, 1).wait()
        return c

    lax.fori_loop(0, td, start, 0)
    lax.fori_loop(0, td, wait, 0)


def _dispatch(pos, h8, xs8, td):
    n = pos.shape[0]
    pos3 = pos.reshape(n // td, 1, 2 * td)
    return pl.pallas_call(
        functools.partial(_dispatch_kernel, td=td),
        grid=(n // td,),
        in_specs=[pl.BlockSpec((1, 1, 2 * td), lambda t: (t, 0, 0), memory_space=pltpu.SMEM),
                  pl.BlockSpec((td * ROW_CHUNKS, LANES), lambda t: (t, 0)),
                  pl.BlockSpec(memory_space=pl.ANY)],
        out_specs=pl.BlockSpec(memory_space=pl.ANY),
        out_shape=jax.ShapeDtypeStruct(xs8.shape, xs8.dtype),
        scratch_shapes=[pltpu.SemaphoreType.DMA(())],
        input_output_aliases={2: 0},
        compiler_params=pltpu.CompilerParams(dimension_semantics=("arbitrary",)),
        name="dispatch",
    )(pos3, h8, xs8)


def _experts_kernel(te_ref, tv_ref, tf_ref, xs_ref, wg_ref, wu_ref, wd_ref, ys_ref,
                    x2_ref, wgb_ref, wub_ref, wdb_ref, *, tm):
    del te_ref
    t = pl.program_id(0)

    @pl.when(tf_ref[t] == 1)
    def _cast_weights():
        wgb_ref[...] = wg_ref[0].astype(BF16)
        wub_ref[...] = wu_ref[0].astype(BF16)
        wdb_ref[...] = wd_ref[0].astype(BF16)

    @pl.when(tv_ref[t] == 1)
    def _compute():
        for j in range(ROW_CHUNKS):
            x2_ref[:, j * LANES:(j + 1) * LANES] = xs_ref[pl.ds(j, tm, stride=ROW_CHUNKS), :].astype(BF16)
        x = x2_ref[...]
        g = jnp.dot(x, wgb_ref[...], preferred_element_type=F32)
        u = jnp.dot(x, wub_ref[...], preferred_element_type=F32)
        act = (g * jax.nn.sigmoid(g) * u).astype(BF16)
        y = jnp.dot(act, wdb_ref[...], preferred_element_type=F32)
        for j in range(ROW_CHUNKS):
            ys_ref[pl.ds(j, tm, stride=ROW_CHUNKS), :] = y[:, j * LANES:(j + 1) * LANES]

    @pl.when(tv_ref[t] == 0)
    def _pad():
        ys_ref[...] = jnp.zeros_like(ys_ref)


def _experts(tile_e, tile_valid, tile_first, xs8, w_eg, w_eu, w_ed, tm):
    n_tiles = tile_e.shape[0]
    grid_spec = pltpu.PrefetchScalarGridSpec(
        num_scalar_prefetch=3,
        grid=(n_tiles,),
        in_specs=[
            pl.BlockSpec((tm * ROW_CHUNKS, LANES), lambda t, te, tv, tf: (t, 0)),
            pl.BlockSpec((1, D_MODEL, D_EXPERT), lambda t, te, tv, tf: (te[t], 0, 0)),
            pl.BlockSpec((1, D_MODEL, D_EXPERT), lambda t, te, tv, tf: (te[t], 0, 0)),
            pl.BlockSpec((1, D_EXPERT, D_MODEL), lambda t, te, tv, tf: (te[t], 0, 0)),
        ],
        out_specs=pl.BlockSpec((tm * ROW_CHUNKS, LANES), lambda t, te, tv, tf: (t, 0)),
        scratch_shapes=[pltpu.VMEM((tm, D_MODEL), BF16), pltpu.VMEM((D_MODEL, D_EXPERT), BF16),
                        pltpu.VMEM((D_MODEL, D_EXPERT), BF16), pltpu.VMEM((D_EXPERT, D_MODEL), BF16)],
    )
    return pl.pallas_call(
        functools.partial(_experts_kernel, tm=tm),
        grid_spec=grid_spec,
        out_shape=jax.ShapeDtypeStruct(xs8.shape, F32),
        compiler_params=pltpu.CompilerParams(
            dimension_semantics=("arbitrary",), vmem_limit_bytes=VMEM_LIMIT),
        name="experts",
    )(tile_e, tile_valid, tile_first, xs8, w_eg, w_eu, w_ed)


def _combine_kernel(pos_ref, x1_ref, wsel_ref, gfin_ref, ys_hbm, o_ref, buf_ref, sem, *, td):
    def copy(r, k):
        return pltpu.make_async_copy(_row_tile(ys_hbm, pos_ref[0, 0, 2 * r + k]), _row_tile(buf_ref.at[k], r), sem)

    def start(r, c):
        copy(r, 0).start(priority=0)
        copy(r, 1).start(priority=1)
        return c

    def wait(r, c):
        copy(r, 0).wait()
        copy(r, 1).wait()
        return c

    lax.fori_loop(0, td, start, 0)
    lax.fori_loop(0, td, wait, 0)

    w = wsel_ref[...]
    w0 = w[:, 0:1]
    w1 = w[:, 1:2]
    cols = []
    for j in range(ROW_CHUNKS):
        cols.append(x1_ref[:, j * LANES:(j + 1) * LANES]
                    + w0 * buf_ref[0, pl.ds(j, td, stride=ROW_CHUNKS), :]
                    + w1 * buf_ref[1, pl.ds(j, td, stride=ROW_CHUNKS), :])
    o_ref[...] = _rms(jnp.concatenate(cols, axis=1), gfin_ref[...])


def _combine_final(pos, x1, wsel, g_final, ys8, td):
    n = pos.shape[0]
    pos3 = pos.reshape(n // td, 1, 2 * td)
    return pl.pallas_call(
        functools.partial(_combine_kernel, td=td),
        grid=(n // td,),
        in_specs=[pl.BlockSpec((1, 1, 2 * td), lambda t: (t, 0, 0), memory_space=pltpu.SMEM),
                  pl.BlockSpec((td, D_MODEL), lambda t: (t, 0)),
                  pl.BlockSpec((td, ROUTER_COLS), lambda t: (t, 0)),
                  pl.BlockSpec((1, D_MODEL), lambda t: (0, 0)),
                  pl.BlockSpec(memory_space=pl.ANY)],
        out_specs=pl.BlockSpec((td, D_MODEL), lambda t: (t, 0)),
        out_shape=jax.ShapeDtypeStruct((n, D_MODEL), F32),
        scratch_shapes=[pltpu.VMEM((2, td * ROW_CHUNKS, LANES), F32), pltpu.SemaphoreType.DMA(())],
        compiler_params=pltpu.CompilerParams(
            dimension_semantics=("arbitrary",), vmem_limit_bytes=VMEM_LIMIT),
        name="combine_final",
    )(pos3, x1, wsel, g_final, ys8)


def _routing_plan(e_sel, tm, n_tiles):
    e_flat = e_sel.reshape(-1)
    onehot = (e_flat[:, None] == jnp.arange(N_EXPERTS, dtype=I32)[None, :]).astype(I32)
    csum = jnp.cumsum(onehot, axis=0)
    rank = jnp.sum(onehot * csum, axis=1) - 1
    counts = csum[-1]
    tiles_per = (counts + tm - 1) // tm
    tile_end = jnp.cumsum(tiles_per)
    tile_start = tile_end - tiles_per
    pos = (tile_start * tm)[e_flat] + rank
    total = tile_end[-1]
    t = jnp.arange(n_tiles, dtype=I32)
    tile_e = jnp.minimum(jnp.sum((t[:, None] >= tile_end[None, :]).astype(I32), axis=1), N_EXPERTS - 1)
    last_e = jnp.max(jnp.where(counts > 0, jnp.arange(N_EXPERTS, dtype=I32), 0))
    valid = t < total
    tile_e = jnp.where(valid, tile_e, last_e).astype(I32)
    first = (valid & (t == tile_start[tile_e])).astype(I32)
    return pos.reshape(-1, 2).astype(I32), tile_e, valid.astype(I32), first


def _block_diag(w):
    nb, d, _ = w.shape
    eye = jnp.eye(nb, dtype=w.dtype)
    return (eye[:, None, :, None] * w[:, :, None, :]).reshape(nb * d, nb * d)


def kernel(x_prompt, x_sample, cache_k, cache_v, state_lru_h, state_conv, page_table, g_mix, w_in, conv_w, conv_b, w_a, b_a, w_i, b_i, lam, b_sb, g_out_lru, g_out_sb, w_out, g_ffn, w_rg, b_rg, w_re, b_re, w_eg, w_eu, w_ed, g_final):
    depth = w_in.shape[0]
    assert depth == 1, "single-layer trunk"
    batch, seq, _ = x_prompt.shape
    nseq, nt, _ = x_sample.shape
    npool, page = cache_k.shape[1], cache_k.shape[2]
    n_p = batch * seq
    n_s = nseq * nt
    row = lambda a: a.reshape(1, -1)

    w_gates = jnp.concatenate([_block_diag(w_a[0]), _block_diag(w_i[0])], axis=1)
    b_gates = jnp.concatenate([b_a[0], b_i[0]]).reshape(1, -1)
    w_r = jnp.zeros((D_MODEL, ROUTER_COLS), F32)
    w_r = w_r.at[:, :N_GROUPS].set(w_rg[0]).at[:, N_GROUPS:N_GROUPS + N_EXPERTS].set(w_re[0])
    b_r = jnp.zeros((1, ROUTER_COLS), F32)
    b_r = b_r.at[0, :N_GROUPS].set(b_rg[0]).at[0, N_GROUPS:N_GROUPS + N_EXPERTS].set(b_re[0])
    lru_tail = (row(lam[0]), row(g_out_lru[0]))

    x_p = x_prompt.reshape(n_p, D_MODEL)
    q_p, kt_p, vt_p, kb_p, vtb_p, ylru_p, ht_p, cbuf_p = _inproj_lru_prompt(
        x_p, row(g_mix[0]), w_in[0].astype(BF16), conv_w[0], row(conv_b[0]), w_gates.astype(BF16), b_gates,
        *lru_tail, batch, seq)
    ysb_p = _sb_prompt(b_sb[0], q_p, kb_p, vtb_p, row(g_out_sb[0]), batch, seq)

    x_s = x_sample.transpose(1, 0, 2).reshape(n_s, D_MODEL)
    buf_tm = state_conv[0].transpose(1, 0, 2).reshape((CONV_W - 1) * nseq, LRU_W)
    q_s, k_s, v_s, ylru_s, ht_s, cbuf_s = _inproj_lru_sample(
        x_s, state_lru_h[0], buf_tm, row(g_mix[0]), w_in[0], conv_w[0], row(conv_b[0]), w_gates, b_gates,
        *lru_tail, nseq, nt)
    seq_major = lambda a, w: a.reshape(-1, nseq, w).transpose(1, 0, 2)
    q_sm, k_sm, v_sm = seq_major(q_s, SB_W), seq_major(k_s, SB_W), seq_major(v_s, SB_W)
    pages_t = lambda c: c[0].transpose(0, 2, 3, 1).reshape(npool, SB_W, page)
    ysb_s = _sb_sample(page_table, b_sb[0], q_sm, k_sm, v_sm, row(g_out_sb[0]),
                       pages_t(cache_k), pages_t(cache_v))
    x_s2 = x_sample.reshape(n_s, D_MODEL)
    ylru_s2 = seq_major(ylru_s, LRU_W).reshape(n_s, LRU_W)
    ysb_s2 = ysb_s.reshape(n_s, SB_W)

    router_args = (row(g_ffn[0]), w_r, b_r)
    x1_p, h8_p, e_p, wsel_p = _outproj_router(x_p, ylru_p, ysb_p, w_out[0].astype(BF16), *router_args, TM_OUT)
    x1_s, h8_s, e_s, wsel_s = _outproj_router(x_s2, ylru_s2, ysb_s2, w_out[0], *router_args, n_s)

    n_all = n_p + n_s
    n_tiles = -(-2 * n_all // TM_EXP) + N_EXPERTS
    e_all = jnp.concatenate([e_p[:, :2], e_s[:, :2]], axis=0)
    pos, tile_e, tile_valid, tile_first = _routing_plan(e_all, TM_EXP, n_tiles)
    pos_p, pos_s = pos[:n_p], pos[n_p:]

    xs8 = jnp.zeros((n_tiles * TM_EXP * ROW_CHUNKS, LANES), F32)
    xs8 = _dispatch(pos_p, h8_p, xs8, TD)
    xs8 = _dispatch(pos_s, h8_s, xs8, n_s)
    ys8 = _experts(tile_e, tile_valid, tile_first, xs8, w_eg[0], w_eu[0], w_ed[0], TM_EXP)

    gfin = row(g_final)
    y_p = _combine_final(pos_p, x1_p, wsel_p, gfin, ys8, TD)
    y_s = _combine_final(pos_s, x1_s, wsel_s, gfin, ys8, n_s)

    heads = lambda a, lead: a.reshape(*lead, HEADS, HEAD_DIM)
    from_t = lambda a: a.reshape(batch, HEADS, HEAD_DIM, seq).transpose(0, 3, 1, 2)[None]
    return (
        y_p.reshape(batch, seq, D_MODEL),
        y_s.reshape(nseq, nt, D_MODEL),
        from_t(kt_p),
        from_t(vt_p),
        ht_p.reshape(1, batch, LRU_W),
        cbuf_p.reshape(1, batch, CONV_W - 1, LRU_W),
        heads(k_sm, (1, nseq, nt)),
        heads(v_sm, (1, nseq, nt)),
        ht_s.reshape(1, nseq, LRU_W),
        cbuf_s.reshape(CONV_W - 1, nseq, LRU_W).transpose(1, 0, 2)[None],
    )
```

```python
import functools
import math

import jax
import jax.numpy as jnp
from jax import lax
from jax.experimental import pallas as pl
from jax.experimental.pallas import tpu as pltpu

F32 = jnp.float32
BF16 = jnp.bfloat16
I32 = jnp.int32

D_MODEL = 1024
LRU_W = 512
SB_W = 512
HEADS = 8
HEAD_DIM = 64
CONV_W = 4
LRU_C = 8.0
EPS = 1e-6
N_GROUPS = 4
EXPERTS_PER_GROUP = 8
N_EXPERTS = 32
D_EXPERT = 512

LANES = 128
SUBLANES = 8
ROW_CHUNKS = D_MODEL // LANES

TM_PROJ = 512
TQ = 256
TM_OUT = 512
TM_EXP = 256
TD = 256
PAGES_PER_STEP = 16
ROUTER_COLS = LANES
VMEM_LIMIT = 56 * 1024 * 1024


def _softplus(z):
    return jnp.maximum(z, 0.0) + jnp.log1p(jnp.exp(-jnp.abs(z)))


def _log_beta_terms(z, accurate):
    t = jnp.exp(-jnp.abs(z))
    log_beta = jnp.minimum(z, 0.0) - (jnp.log1p(t) if accurate else jnp.log(1.0 + t))
    return log_beta - z, log_beta


def _rms(x, g):
    return x * lax.rsqrt(jnp.mean(x * x, axis=-1, keepdims=True) + EPS) * g


def _gelu_tanh(x):
    c = math.sqrt(2.0 / math.pi)
    return x * (0.5 * (1.0 + jnp.tanh(c * (x + 0.044715 * (x * x * x)))))


def _neg_expm1(x):
    u = jnp.exp(x)
    near = x > -1.0
    ratio = x / jnp.where(near & (u < 1.0), jnp.log(u), -1.0)
    return jnp.where(near, jnp.where(u < 1.0, (1.0 - u) * ratio, -x), 1.0 - u)


def _lru_gates(u, gates, lam):
    r = jax.nn.sigmoid(gates[:, :LRU_W])
    i = jax.nn.sigmoid(gates[:, LRU_W:])
    log_a = (-LRU_C) * r * _softplus(-lam)
    a = jnp.exp(log_a)
    xin = jnp.sqrt(_neg_expm1(2.0 * log_a)) * (i * u)
    return a, xin


def _split_bf16(x):
    hi = x.astype(BF16)
    lo = (x - hi.astype(F32)).astype(BF16)
    return hi, lo


def _dot3(x, w):
    x_hi, x_lo = _split_bf16(x)
    w_hi, w_lo = _split_bf16(w)
    n = x.shape[0]
    top = jnp.dot(jnp.concatenate([x_hi, x_lo], axis=0), w_hi, preferred_element_type=F32)
    return top[:n] + top[n:] + jnp.dot(x_hi, w_lo, preferred_element_type=F32)


def _inproj_lru_kernel(x_ref, gmix_ref, win_ref, cw_ref, cb_ref, wg_ref, bg_ref, lam_ref, glru_ref,
                       q_ref, kt_ref, vt_ref, kb_ref, vtb_ref, ylru_ref, ht_ref, cbuf_ref,
                       xlp_ref, a_ref, s_ref, h_ref, *, tm, tk):
    t = pl.program_id(1)
    nt = pl.num_programs(1)

    @pl.when(t == 0)
    def _init():
        xlp_ref[0:SUBLANES, :] = jnp.zeros((SUBLANES, LRU_W), F32)
        h_ref[...] = jnp.zeros_like(h_ref)

    hn = _rms(x_ref[...], gmix_ref[...]).astype(BF16)

    def proj(c):
        return jnp.dot(hn, win_ref[:, c * LRU_W:(c + 1) * LRU_W], preferred_element_type=F32)

    xl = proj(0)
    gl = proj(1)
    q_ref[...] = proj(2).astype(BF16)
    k = proj(3)
    kt_ref[0] = k.T
    kb_ref[...] = k.astype(BF16)
    v_t = proj(4).T
    vt_ref[0] = v_t
    for c in range(tm // tk):
        vtb_ref[0, c] = v_t[:, c * tk:(c + 1) * tk].astype(BF16)

    xlp_ref[SUBLANES:SUBLANES + tm, :] = xl
    cw = cw_ref[...]
    u = cb_ref[...] + cw[3:4, :] * xl
    for j in range(1, CONV_W):
        u = u + cw[3 - j:4 - j, :] * xlp_ref[pl.ds(SUBLANES - j, tm), :]
    xlp_ref[0:SUBLANES, :] = xl[tm - SUBLANES:tm, :]

    gates = jnp.dot(u.astype(BF16), wg_ref[...], preferred_element_type=F32) + bg_ref[...]
    a, xin = _lru_gates(u, gates, lam_ref[...])
    a_ref[...] = a
    s_ref[...] = xin

    row = lax.broadcasted_iota(I32, (SUBLANES, LRU_W), 0)

    def group(g, h):
        r0 = pl.multiple_of(g * SUBLANES, SUBLANES)
        av = a_ref[pl.ds(r0, SUBLANES), :]
        xv = s_ref[pl.ds(r0, SUBLANES), :]
        for d in (1, 2, 4):
            valid = row >= d
            xv = jnp.where(valid, xv + av * pltpu.roll(xv, d, 0), xv)
            av = jnp.where(valid, av * pltpu.roll(av, d, 0), av)
        hv = xv + av * h
        s_ref[pl.ds(r0, SUBLANES), :] = hv
        return hv[SUBLANES - 1:SUBLANES, :]

    h = lax.fori_loop(0, tm // SUBLANES, group, h_ref[0:1, :])
    h_ref[0:1, :] = h

    y = s_ref[...] * _gelu_tanh(gl)
    ylru_ref[...] = _rms(y, glru_ref[...]).astype(BF16)

    @pl.when(t == nt - 1)
    def _fin():
        ht_ref[0] = h
        cbuf_ref[0] = xl[tm - (CONV_W - 1):tm, :]


def _inproj_lru_prompt(x2, g_mix, w_in_bf, conv_w, conv_b, w_gates, b_gates, lam, g_lru, batch, seq):
    tm, tk = TM_PROJ, TQ
    nt = seq // tm
    n = batch * seq
    row_spec = lambda w: pl.BlockSpec((tm, w), lambda b, t: (b * nt + t, 0))
    full = lambda a: pl.BlockSpec(a.shape, lambda b, t: (0,) * a.ndim)
    t_spec = pl.BlockSpec((1, SB_W, tm), lambda b, t: (b, 0, t))
    out_shape = (
        jax.ShapeDtypeStruct((n, SB_W), BF16),
        jax.ShapeDtypeStruct((batch, SB_W, seq), F32),
        jax.ShapeDtypeStruct((batch, SB_W, seq), F32),
        jax.ShapeDtypeStruct((n, SB_W), BF16),
        jax.ShapeDtypeStruct((batch, seq // tk, SB_W, tk), BF16),
        jax.ShapeDtypeStruct((n, LRU_W), BF16),
        jax.ShapeDtypeStruct((batch, 1, LRU_W), F32),
        jax.ShapeDtypeStruct((batch, CONV_W - 1, LRU_W), F32),
    )
    out_specs = (
        row_spec(SB_W), t_spec, t_spec, row_spec(SB_W),
        pl.BlockSpec((1, tm // tk, SB_W, tk), lambda b, t: (b, t, 0, 0)),
        row_spec(LRU_W),
        pl.BlockSpec((1, 1, LRU_W), lambda b, t: (b, 0, 0)),
        pl.BlockSpec((1, CONV_W - 1, LRU_W), lambda b, t: (b, 0, 0)),
    )
    args = (x2, g_mix, w_in_bf, conv_w, conv_b, w_gates, b_gates, lam, g_lru)
    return pl.pallas_call(
        functools.partial(_inproj_lru_kernel, tm=tm, tk=tk),
        grid=(batch, nt),
        in_specs=[row_spec(D_MODEL)] + [full(a) for a in args[1:]],
        out_specs=out_specs,
        out_shape=out_shape,
        scratch_shapes=[
            pltpu.VMEM((tm + SUBLANES, LRU_W), F32),
            pltpu.VMEM((tm, LRU_W), F32),
            pltpu.VMEM((tm, LRU_W), F32),
            pltpu.VMEM((SUBLANES, LRU_W), F32),
        ],
        compiler_params=pltpu.CompilerParams(
            dimension_semantics=("arbitrary", "arbitrary"), vmem_limit_bytes=VMEM_LIMIT),
        name="inproj_lru_prompt",
    )(*args)


def _inproj_lru_sample_kernel(x_ref, h0_ref, buf_ref, gmix_ref, win_ref, cw_ref, cb_ref, wg_ref, bg_ref,
                              lam_ref, glru_ref, q_ref, k_ref, v_ref, ylru_ref, ht_ref, cbuf_ref,
                              *, nseq, nt):
    hn = _rms(x_ref[...], gmix_ref[...])

    def proj(c):
        return _dot3(hn, win_ref[:, c * LRU_W:(c + 1) * LRU_W])

    xl = proj(0)
    gl = proj(1)
    q_ref[...] = proj(2)
    k_ref[...] = proj(3)
    v_ref[...] = proj(4)

    hist = [buf_ref[m * nseq:(m + 1) * nseq, :] for m in range(CONV_W - 1)]
    hist += [xl[t * nseq:(t + 1) * nseq, :] for t in range(nt)]
    cw = cw_ref[...]
    us = []
    for t in range(nt):
        ut = cb_ref[...] + cw[0:1, :] * hist[t]
        for j in range(1, CONV_W):
            ut = ut + cw[j:j + 1, :] * hist[t + j]
        us.append(ut)
    u = jnp.concatenate(us, axis=0)
    gates = _dot3(u, wg_ref[...]) + bg_ref[...]
    a, xin = _lru_gates(u, gates, lam_ref[...])
    h = h0_ref[...]
    hs = []
    for t in range(nt):
        h = a[t * nseq:(t + 1) * nseq, :] * h + xin[t * nseq:(t + 1) * nseq, :]
        hs.append(h)
    y = jnp.concatenate(hs, axis=0) * _gelu_tanh(gl)
    ylru_ref[...] = _rms(y, glru_ref[...])
    ht_ref[...] = h
    cbuf_ref[...] = jnp.concatenate(hist[nt:nt + CONV_W - 1], axis=0)


def _inproj_lru_sample(x_tm, h0, buf_tm, g_mix, w_in, conv_w, conv_b, w_gates, b_gates, lam, g_lru,
                       nseq, nt):
    n = nseq * nt
    out_shape = (
        jax.ShapeDtypeStruct((n, SB_W), F32),
        jax.ShapeDtypeStruct((n, SB_W), F32),
        jax.ShapeDtypeStruct((n, SB_W), F32),
        jax.ShapeDtypeStruct((n, LRU_W), F32),
        jax.ShapeDtypeStruct((nseq, LRU_W), F32),
        jax.ShapeDtypeStruct(((CONV_W - 1) * nseq, LRU_W), F32),
    )
    return pl.pallas_call(
        functools.partial(_inproj_lru_sample_kernel, nseq=nseq, nt=nt),
        out_shape=out_shape,
        compiler_params=pltpu.CompilerParams(vmem_limit_bytes=VMEM_LIMIT),
        name="inproj_lru_sample",
    )(x_tm, h0, buf_tm, g_mix, w_in, conv_w, conv_b, w_gates, b_gates, lam, g_lru)


def _sb_prompt_kernel(bsb_ref, q_ref, k_ref, vt_ref, g_ref, o_ref, acc_ref, c_ref, qm_ref, *, tq):
    i = pl.program_id(1)
    tk = tq
    rowk = lax.broadcasted_iota(I32, (tk, tq), 0)
    colq = lax.broadcasted_iota(I32, (tk, tq), 1)
    diag_mask = rowk < colq
    later_mat = (lax.broadcasted_iota(I32, (tk, tk), 1) > lax.broadcasted_iota(I32, (tk, tk), 0)).astype(BF16)
    lane = lax.broadcasted_iota(I32, (tq, LANES), 1)
    scale = jnp.asarray(HEAD_DIM ** -0.5, BF16)
    heads_per_tile = LANES // HEAD_DIM

    for h in range(HEADS):
        pair, half = divmod(h, heads_per_tile)
        in_head = (lane >= half * HEAD_DIM) & (lane < (half + 1) * HEAD_DIM)
        qm_ref[h] = jnp.where(in_head, q_ref[:, pair * LANES:(pair + 1) * LANES] * scale, jnp.zeros((), BF16))
    acc_ref[...] = jnp.zeros_like(acc_ref)
    c_ref[...] = jnp.zeros_like(c_ref)

    def block(j, masked):
        k0 = pl.multiple_of(j * tk, tk)
        kbs = [k_ref[pl.ds(k0, tk), p * LANES:(p + 1) * LANES] for p in range(HEADS // heads_per_tile)]
        ss = [lax.dot_general(kbs[h // heads_per_tile], qm_ref[h], (((1,), (1,)), ((), ())),
                              preferred_element_type=F32) + bsb_ref[h] for h in range(HEADS)]
        l1s, lbs = [], []
        for h in range(HEADS):
            l1, lb = _log_beta_terms(ss[h], accurate=False)
            if masked:
                l1 = jnp.where(diag_mask, l1, 0.0)
            l1s.append(l1.astype(BF16))
            lbs.append(lb)
        exs = [jnp.dot(later_mat, l1s[h], preferred_element_type=F32) for h in range(HEADS)]
        ws = []
        for h in range(HEADS):
            w = jnp.exp(lbs[h] + exs[h] + c_ref[h:h + 1, :])
            if masked:
                w = jnp.where(diag_mask, w, 0.0)
            ws.append(w.astype(BF16))
        for h in range(HEADS):
            rows = slice(h * HEAD_DIM, (h + 1) * HEAD_DIM)
            acc_ref[rows, :] += jnp.dot(vt_ref[0, j, rows, :], ws[h], preferred_element_type=F32)
            c_ref[h:h + 1, :] += exs[h][0:1, :] + l1s[h][0:1, :]

    block(i, True)

    def body(jj, carry):
        block(i - 1 - jj, False)
        return carry

    lax.fori_loop(0, i, body, 0)
    o_ref[...] = _rms(acc_ref[...].T, g_ref[...]).astype(BF16)


def _sb_prompt(b_sb, q_bf, k_bf, vt_bf, g_sb, batch, seq):
    tq = TQ
    nq = seq // tq
    n = batch * seq
    return pl.pallas_call(
        functools.partial(_sb_prompt_kernel, tq=tq),
        grid=(batch, nq),
        in_specs=[
            pl.BlockSpec(memory_space=pltpu.SMEM),
            pl.BlockSpec((tq, SB_W), lambda b, i: (b * nq + i, 0)),
            pl.BlockSpec((seq, SB_W), lambda b, i: (b, 0)),
            pl.BlockSpec((1, nq, SB_W, tq), lambda b, i: (b, 0, 0, 0)),
            pl.BlockSpec((1, SB_W), lambda b, i: (0, 0)),
        ],
        out_specs=pl.BlockSpec((tq, SB_W), lambda b, i: (b * nq + i, 0)),
        out_shape=jax.ShapeDtypeStruct((n, SB_W), BF16),
        scratch_shapes=[pltpu.VMEM((SB_W, tq), F32), pltpu.VMEM((HEADS, tq), F32),
                        pltpu.VMEM((HEADS, tq, LANES), BF16)],
        compiler_params=pltpu.CompilerParams(
            dimension_semantics=("arbitrary", "arbitrary"), vmem_limit_bytes=VMEM_LIMIT),
        name="sb_prompt",
    )(b_sb, q_bf, k_bf, vt_bf, g_sb)


def _sb_sample_kernel(pt_ref, bsb_ref, q_ref, kown_ref, vown_ref, g_ref, *rest, npg, nt, page):
    kp = rest[:npg]
    vp = rest[npg:2 * npg]
    o_ref = rest[2 * npg]
    qbd_ref, acc_ref, c_ref = rest[2 * npg + 1:]
    ci = pl.program_id(1)
    nc = pl.num_programs(1)
    rows = nt * HEADS
    rowi = lax.broadcasted_iota(I32, (rows, 1), 0)
    row_h = rowi & (HEADS - 1)
    row_t = rowi >> int(math.log2(HEADS))
    bias = jnp.zeros((rows, 1), F32)
    for h in range(HEADS):
        bias = jnp.where(row_h == h, bsb_ref[h], bias)
    head_of_lane = lax.broadcasted_iota(I32, (HEADS, SB_W), 1) // HEAD_DIM
    head_mask = head_of_lane == lax.broadcasted_iota(I32, (HEADS, SB_W), 0)

    @pl.when(ci == 0)
    def _init():
        q = q_ref[0] * (HEAD_DIM ** -0.5)
        qbd = jnp.concatenate(
            [jnp.where(head_mask, jnp.broadcast_to(q[t:t + 1, :], (HEADS, SB_W)), 0.0) for t in range(nt)],
            axis=0)
        qbd_ref[...] = jnp.concatenate(_split_bf16(qbd), axis=0)
        kown = kown_ref[0]
        vown = vown_ref[0]
        c = jnp.zeros((rows, 1), F32)
        acc = jnp.zeros((rows, SB_W), F32)
        for s in range(nt - 1, -1, -1):
            z = jnp.sum(qbd * kown[s:s + 1, :], axis=-1, keepdims=True) + bias
            vis = s < row_t
            l1, lb = _log_beta_terms(z, accurate=True)
            l1 = jnp.where(vis, l1, 0.0)
            w = jnp.where(vis, jnp.exp(lb + c), 0.0)
            acc = acc + w * vown[s:s + 1, :]
            c = c + l1
        acc_ref[...] = acc
        c_ref[...] = jnp.broadcast_to(c, (rows, LANES))

    later_mat = (lax.broadcasted_iota(I32, (page, page), 0) > lax.broadcasted_iota(I32, (page, page), 1)).astype(BF16)
    nt_dims = (((1,), (1,)), ((), ()))
    q2 = qbd_ref[...]
    q_hi = q2[:rows]

    def fold(x):
        return x[:rows] + x[rows:]

    ss = []
    for i in range(npg):
        kt_hi, kt_lo = _split_bf16(kp[i][0])
        ss.append(fold(jnp.dot(q2, kt_hi, preferred_element_type=F32))
                  + jnp.dot(q_hi, kt_lo, preferred_element_type=F32) + bias)
    l1s, lbs, exs = [], [], []
    for i in range(npg):
        l1, lb = _log_beta_terms(ss[i], accurate=True)
        l1s.append(l1)
        lbs.append(lb)
        exs.append(fold(jnp.dot(jnp.concatenate(_split_bf16(l1), axis=0), later_mat,
                                preferred_element_type=F32)))
    c = c_ref[...]
    acc = acc_ref[...]
    for i in range(npg):
        w = jnp.exp(lbs[i] + exs[i] + c)
        w_hi, w_lo = _split_bf16(w)
        vt_hi, vt_lo = _split_bf16(vp[i][0])
        acc = (acc + fold(lax.dot_general(jnp.concatenate([w_hi, w_lo], axis=0), vt_hi, nt_dims,
                                          preferred_element_type=F32))
               + lax.dot_general(w_hi, vt_lo, nt_dims, preferred_element_type=F32))
        c = c + jnp.broadcast_to(exs[i][:, 0:1] + l1s[i][:, 0:1], (rows, LANES))
    acc_ref[...] = acc
    c_ref[...] = c

    @pl.when(ci == nc - 1)
    def _fin():
        ys = [jnp.sum(jnp.where(head_mask, acc[t * HEADS:(t + 1) * HEADS, :], 0.0), axis=0, keepdims=True)
              for t in range(nt)]
        o_ref[0] = _rms(jnp.concatenate(ys, axis=0), g_ref[...])


def _sb_sample(page_table, b_sb, q_s, k_own, v_own, g_sb, cache_kt, cache_vt):
    nseq, npages = page_table.shape
    nt = q_s.shape[1]
    page = cache_kt.shape[2]
    npg = PAGES_PER_STEP
    nc = npages // npg
    rows = nt * HEADS

    def page_spec(i):
        return pl.BlockSpec(
            (1, SB_W, page),
            lambda s, c, pt, i=i: (pt[s * npages + (npages - 1) - (c * npg + i)], 0, 0))

    seq_spec = pl.BlockSpec((1, nt, SB_W), lambda s, c, pt: (s, 0, 0))
    grid_spec = pltpu.PrefetchScalarGridSpec(
        num_scalar_prefetch=1,
        grid=(nseq, nc),
        in_specs=[pl.BlockSpec(memory_space=pltpu.SMEM), seq_spec, seq_spec, seq_spec,
                  pl.BlockSpec((1, SB_W), lambda s, c, pt: (0, 0))]
                 + [page_spec(i) for i in range(npg)] + [page_spec(i) for i in range(npg)],
        out_specs=pl.BlockSpec((1, nt, SB_W), lambda s, c, pt: (s, 0, 0)),
        scratch_shapes=[pltpu.VMEM((2 * rows, SB_W), BF16), pltpu.VMEM((rows, SB_W), F32),
                        pltpu.VMEM((rows, LANES), F32)],
    )
    return pl.pallas_call(
        functools.partial(_sb_sample_kernel, npg=npg, nt=nt, page=page),
        grid_spec=grid_spec,
        out_shape=jax.ShapeDtypeStruct((nseq, nt, SB_W), F32),
        compiler_params=pltpu.CompilerParams(
            dimension_semantics=("arbitrary", "arbitrary"), vmem_limit_bytes=VMEM_LIMIT),
        name="sb_sample",
    )(page_table.reshape(-1), b_sb, q_s, k_own, v_own, g_sb, *([cache_kt] * npg), *([cache_vt] * npg))


def _outproj_router_kernel(x_ref, yl_ref, ys_ref, wout_ref, gffn_ref, wr_ref, br_ref, *rest, tm, precise):
    x1_ref, h8_ref, e_ref, w_ref = rest[-4:]
    lru_rows = slice(0, LRU_W)
    sb_rows = slice(LRU_W, LRU_W + SB_W)
    if precise:
        x1 = (x_ref[...] + _dot3(yl_ref[...], wout_ref[lru_rows, :]) + _dot3(ys_ref[...], wout_ref[sb_rows, :]))
    else:
        x1 = (x_ref[...]
              + jnp.dot(yl_ref[...], wout_ref[lru_rows, :], preferred_element_type=F32)
              + jnp.dot(ys_ref[...], wout_ref[sb_rows, :], preferred_element_type=F32))
    x1_ref[...] = x1
    h = _rms(x1, gffn_ref[...])
    for j in range(ROW_CHUNKS):
        h8_ref[pl.ds(j, tm, stride=ROW_CHUNKS), :] = h[:, j * LANES:(j + 1) * LANES]

    logits = _dot3(h, wr_ref[...]) + br_ref[...]

    lane = lax.broadcasted_iota(I32, (tm, ROUTER_COLS), 1)
    big = jnp.asarray(ROUTER_COLS, I32)
    neg = jnp.asarray(-jnp.inf, F32)
    gl = jnp.where(lane < N_GROUPS, logits, neg)
    gmax = jnp.max(gl, axis=-1, keepdims=True)
    g_idx = jnp.min(jnp.where(gl == gmax, lane, big), axis=-1, keepdims=True)
    p_g = 1.0 / jnp.sum(jnp.exp(gl - gmax), axis=-1, keepdims=True)
    e_lo = N_GROUPS + g_idx * EXPERTS_PER_GROUP
    el = jnp.where((lane >= e_lo) & (lane < e_lo + EXPERTS_PER_GROUP), logits, neg)
    m1 = jnp.max(el, axis=-1, keepdims=True)
    i1 = jnp.min(jnp.where(el == m1, lane, big), axis=-1, keepdims=True)
    el2 = jnp.where(lane == i1, neg, el)
    m2 = jnp.max(el2, axis=-1, keepdims=True)
    i2 = jnp.min(jnp.where(el2 == m2, lane, big), axis=-1, keepdims=True)
    r = jnp.exp(m2 - m1)
    w1 = p_g / (1.0 + r)
    w2 = w1 * r
    e_ref[...] = jnp.where(lane == 0, i1 - N_GROUPS, jnp.where(lane == 1, i2 - N_GROUPS, 0))
    w_ref[...] = jnp.where(lane == 0, w1, jnp.where(lane == 1, w2, 0.0))


def _outproj_router(x2, ylru, ysb, w_out, g_ffn, w_r, b_r, tm, h8_buf, row0):
    n = x2.shape[0]
    n_all = h8_buf.shape[0] // ROW_CHUNKS
    precise = ylru.dtype == F32
    assert ysb.dtype == ylru.dtype == w_out.dtype and row0 % tm == 0
    blk0 = row0 // tm
    row_spec = lambda w: pl.BlockSpec((tm, w), lambda t: (t, 0))
    full = lambda a: pl.BlockSpec(a.shape, lambda t: (0,) * a.ndim)
    args = [x2, ylru, ysb, w_out, g_ffn, w_r, b_r, h8_buf]
    in_specs = [row_spec(D_MODEL), row_spec(LRU_W), row_spec(SB_W), full(w_out), full(g_ffn), full(w_r),
                full(b_r), pl.BlockSpec(memory_space=pl.ANY)]
    aliases = {len(args) - 1: 1}
    return pl.pallas_call(
        functools.partial(_outproj_router_kernel, tm=tm, precise=precise),
        grid=(n // tm,),
        in_specs=in_specs,
        out_specs=(row_spec(D_MODEL), pl.BlockSpec((tm * ROW_CHUNKS, LANES), lambda t: (blk0 + t, 0)),
                   row_spec(ROUTER_COLS), row_spec(ROUTER_COLS)),
        out_shape=(jax.ShapeDtypeStruct((n, D_MODEL), F32),
                   jax.ShapeDtypeStruct((n_all * ROW_CHUNKS, LANES), F32),
                   jax.ShapeDtypeStruct((n, ROUTER_COLS), I32),
                   jax.ShapeDtypeStruct((n, ROUTER_COLS), F32)),
        input_output_aliases=aliases,
        compiler_params=pltpu.CompilerParams(
            dimension_semantics=("arbitrary",), vmem_limit_bytes=VMEM_LIMIT),
        name="outproj_router",
    )(*args)


def _row_tile(ref, row):
    return ref.at[pl.ds(pl.multiple_of(row * ROW_CHUNKS, ROW_CHUNKS), ROW_CHUNKS)]


def _experts_kernel(te_ref, tv_ref, tf_ref, gsrc_ref, gnext_ref, sdst_ref, h8_hbm, wg_ref, wu_ref, wd_ref,
                    out_hbm, xbuf_ref, obuf_ref, x2_ref, wgb_ref, wub_ref, wdb_ref, gsem, ssem, *, tm):
    del te_ref
    t = pl.program_id(0)
    nt = pl.num_programs(0)
    slot = lax.rem(t, 2)
    other = 1 - slot

    def gather(idx_ref, s):
        return lambda r: pltpu.make_async_copy(
            _row_tile(h8_hbm, idx_ref[0, 0, r]), _row_tile(xbuf_ref.at[s], r), gsem.at[s])

    def scatter(s):
        return lambda r: pltpu.make_async_copy(
            _row_tile(obuf_ref.at[s], r), _row_tile(out_hbm, sdst_ref[0, 0, r]), ssem.at[s])

    def start_all(copy):
        def body(r, c):
            copy(r).start()
            return c
        lax.fori_loop(0, tm, body, 0, unroll=8)

    tile_rows = pl.ds(0, tm * ROW_CHUNKS)

    def wait_gather(s):
        pltpu.make_async_copy(h8_hbm.at[tile_rows], xbuf_ref.at[s], gsem.at[s]).wait()

    def wait_scatter(s):
        pltpu.make_async_copy(obuf_ref.at[s], out_hbm.at[tile_rows], ssem.at[s]).wait()

    @pl.when(t == 0)
    def _first_gather():
        start_all(gather(gsrc_ref, slot))

    @pl.when(t + 1 < nt)
    def _next_gather():
        start_all(gather(gnext_ref, other))

    wait_gather(slot)

    @pl.when(tf_ref[t] == 1)
    def _cast_weights():
        wgb_ref[...] = wg_ref[0].astype(BF16)
        wub_ref[...] = wu_ref[0].astype(BF16)
        wdb_ref[...] = wd_ref[0].astype(BF16)

    @pl.when(tv_ref[t] == 1)
    def _compute():
        for j in range(ROW_CHUNKS):
            x2_ref[:, j * LANES:(j + 1) * LANES] = xbuf_ref[slot, pl.ds(j, tm, stride=ROW_CHUNKS), :].astype(BF16)
        x = x2_ref[...]
        g = jnp.dot(x, wgb_ref[...], preferred_element_type=F32)
        u = jnp.dot(x, wub_ref[...], preferred_element_type=F32)
        act = (g * jax.nn.sigmoid(g) * u).astype(BF16)
        y = jnp.dot(act, wdb_ref[...], preferred_element_type=F32)
        for j in range(ROW_CHUNKS):
            obuf_ref[slot, pl.ds(j, tm, stride=ROW_CHUNKS), :] = y[:, j * LANES:(j + 1) * LANES]

    @pl.when(tv_ref[t] == 0)
    def _pad():
        obuf_ref[slot] = jnp.zeros(obuf_ref.shape[1:], F32)

    @pl.when(t >= 1)
    def _prev_scatter_done():
        wait_scatter(other)

    start_all(scatter(slot))

    @pl.when(t == nt - 1)
    def _last_scatter_done():
        wait_scatter(slot)


def _experts(tile_e, tile_valid, tile_first, gsrc, sdst, h8_all, w_eg, w_eu, w_ed, tm, out_rows):
    n_tiles = tile_e.shape[0]
    idx_block = (1, 1, tm)
    grid_spec = pltpu.PrefetchScalarGridSpec(
        num_scalar_prefetch=3,
        grid=(n_tiles,),
        in_specs=[
            pl.BlockSpec(idx_block, lambda t, te, tv, tf: (t, 0, 0), memory_space=pltpu.SMEM),
            pl.BlockSpec(idx_block, lambda t, te, tv, tf: (jnp.minimum(t + 1, n_tiles - 1), 0, 0),
                         memory_space=pltpu.SMEM),
            pl.BlockSpec(idx_block, lambda t, te, tv, tf: (t, 0, 0), memory_space=pltpu.SMEM),
            pl.BlockSpec(memory_space=pl.ANY),
            pl.BlockSpec((1, D_MODEL, D_EXPERT), lambda t, te, tv, tf: (te[t], 0, 0)),
            pl.BlockSpec((1, D_MODEL, D_EXPERT), lambda t, te, tv, tf: (te[t], 0, 0)),
            pl.BlockSpec((1, D_EXPERT, D_MODEL), lambda t, te, tv, tf: (te[t], 0, 0)),
        ],
        out_specs=pl.BlockSpec(memory_space=pl.ANY),
        scratch_shapes=[pltpu.VMEM((2, tm * ROW_CHUNKS, LANES), F32), pltpu.VMEM((2, tm * ROW_CHUNKS, LANES), F32),
                        pltpu.VMEM((tm, D_MODEL), BF16), pltpu.VMEM((D_MODEL, D_EXPERT), BF16),
                        pltpu.VMEM((D_MODEL, D_EXPERT), BF16), pltpu.VMEM((D_EXPERT, D_MODEL), BF16),
                        pltpu.SemaphoreType.DMA((2,)), pltpu.SemaphoreType.DMA((2,))],
    )
    return pl.pallas_call(
        functools.partial(_experts_kernel, tm=tm),
        grid_spec=grid_spec,
        out_shape=jax.ShapeDtypeStruct((out_rows * ROW_CHUNKS, LANES), F32),
        compiler_params=pltpu.CompilerParams(
            dimension_semantics=("arbitrary",), vmem_limit_bytes=VMEM_LIMIT),
        name="experts",
    )(tile_e, tile_valid, tile_first, gsrc, gsrc, sdst, h8_all, w_eg, w_eu, w_ed)


def _combine_kernel(x1_ref, wsel_ref, gfin_ref, y0_ref, y1_ref, o_ref, *, td):
    w = wsel_ref[...]
    w0 = w[:, 0:1]
    w1 = w[:, 1:2]
    cols = []
    for j in range(ROW_CHUNKS):
        cols.append(x1_ref[:, j * LANES:(j + 1) * LANES]
                    + w0 * y0_ref[pl.ds(j, td, stride=ROW_CHUNKS), :]
                    + w1 * y1_ref[pl.ds(j, td, stride=ROW_CHUNKS), :])
    o_ref[...] = _rms(jnp.concatenate(cols, axis=1), gfin_ref[...])


def _combine_final(x1, wsel, g_final, out2, td, row_slot0, row_slot1):
    n = x1.shape[0]
    assert row_slot0 % td == 0 and row_slot1 % td == 0
    blk0, blk1 = row_slot0 // td, row_slot1 // td
    tile_spec = lambda b0: pl.BlockSpec((td * ROW_CHUNKS, LANES), lambda t: (b0 + t, 0))
    return pl.pallas_call(
        functools.partial(_combine_kernel, td=td),
        grid=(n // td,),
        in_specs=[pl.BlockSpec((td, D_MODEL), lambda t: (t, 0)),
                  pl.BlockSpec((td, ROUTER_COLS), lambda t: (t, 0)),
                  pl.BlockSpec((1, D_MODEL), lambda t: (0, 0)),
                  tile_spec(blk0), tile_spec(blk1)],
        out_specs=pl.BlockSpec((td, D_MODEL), lambda t: (t, 0)),
        out_shape=jax.ShapeDtypeStruct((n, D_MODEL), F32),
        compiler_params=pltpu.CompilerParams(
            dimension_semantics=("arbitrary",), vmem_limit_bytes=VMEM_LIMIT),
        name="combine_final",
    )(x1, wsel, g_final, out2, out2)


def _routing_plan(e_sel, tm, n_tiles, n_p, n_s):
    n2 = e_sel.size
    e_flat = e_sel.reshape(-1)
    order = jnp.argsort(e_flat, stable=True).astype(I32)
    experts = jnp.arange(N_EXPERTS, dtype=I32)
    counts = jnp.sum((e_flat[:, None] == experts[None, :]).astype(I32), axis=0)
    tiles_per = (counts + tm - 1) // tm
    tile_end = jnp.cumsum(tiles_per)
    tile_start = tile_end - tiles_per
    packed_start = jnp.cumsum(counts) - counts
    total = tile_end[-1]
    t = jnp.arange(n_tiles, dtype=I32)
    tile_e = jnp.minimum(jnp.sum((t[:, None] >= tile_end[None, :]).astype(I32), axis=1), N_EXPERTS - 1)
    last_e = jnp.max(jnp.where(counts > 0, experts, 0))
    valid = t < total
    tile_e = jnp.where(valid, tile_e, last_e).astype(I32)
    first = (valid & (t == tile_start[tile_e])).astype(I32)

    p = jnp.arange(n_tiles * tm, dtype=I32)
    e_p = jnp.minimum(jnp.sum((p[:, None] >= (tile_end * tm)[None, :]).astype(I32), axis=1), N_EXPERTS - 1)
    is_e = e_p[:, None] == experts[None, :]
    pick = lambda table: jnp.sum(jnp.where(is_e, table[None, :], 0), axis=1)
    local = p - pick(tile_start) * tm
    row_valid = (p < total * tm) & (local < pick(counts))
    f = order[jnp.clip(pick(packed_start) + local, 0, n2 - 1)]
    token, slot = f >> 1, f & 1
    gsrc = jnp.where(row_valid, token, 0).astype(I32)
    dst = jnp.where(token < n_p, slot * n_p + token, 2 * n_p + slot * n_s + (token - n_p))
    sdst = jnp.where(row_valid, dst, 2 * (n_p + n_s) + (p & (tm - 1))).astype(I32)
    shape3 = (n_tiles, 1, tm)
    return gsrc.reshape(shape3), sdst.reshape(shape3), tile_e, valid.astype(I32), first


def _block_diag(w):
    nb, d, _ = w.shape
    eye = jnp.eye(nb, dtype=w.dtype)
    return (eye[:, None, :, None] * w[:, :, None, :]).reshape(nb * d, nb * d)


def kernel(x_prompt, x_sample, cache_k, cache_v, state_lru_h, state_conv, page_table, g_mix, w_in, conv_w, conv_b, w_a, b_a, w_i, b_i, lam, b_sb, g_out_lru, g_out_sb, w_out, g_ffn, w_rg, b_rg, w_re, b_re, w_eg, w_eu, w_ed, g_final):
    depth = w_in.shape[0]
    assert depth == 1, "single-layer trunk"
    batch, seq, _ = x_prompt.shape
    nseq, nt, _ = x_sample.shape
    npool, page = cache_k.shape[1], cache_k.shape[2]
    n_p = batch * seq
    n_s = nseq * nt
    row = lambda a: a.reshape(1, -1)

    w_gates = jnp.concatenate([_block_diag(w_a[0]), _block_diag(w_i[0])], axis=1)
    b_gates = jnp.concatenate([b_a[0], b_i[0]]).reshape(1, -1)
    w_r = jnp.zeros((D_MODEL, ROUTER_COLS), F32)
    w_r = w_r.at[:, :N_GROUPS].set(w_rg[0]).at[:, N_GROUPS:N_GROUPS + N_EXPERTS].set(w_re[0])
    b_r = jnp.zeros((1, ROUTER_COLS), F32)
    b_r = b_r.at[0, :N_GROUPS].set(b_rg[0]).at[0, N_GROUPS:N_GROUPS + N_EXPERTS].set(b_re[0])
    lru_tail = (row(lam[0]), row(g_out_lru[0]))

    x_p = x_prompt.reshape(n_p, D_MODEL)
    q_p, kt_p, vt_p, kb_p, vtb_p, ylru_p, ht_p, cbuf_p = _inproj_lru_prompt(
        x_p, row(g_mix[0]), w_in[0].astype(BF16), conv_w[0], row(conv_b[0]), w_gates.astype(BF16), b_gates,
        *lru_tail, batch, seq)
    ysb_p = _sb_prompt(b_sb[0], q_p, kb_p, vtb_p, row(g_out_sb[0]), batch, seq)

    x_s = x_sample.transpose(1, 0, 2).reshape(n_s, D_MODEL)
    buf_tm = state_conv[0].transpose(1, 0, 2).reshape((CONV_W - 1) * nseq, LRU_W)
    q_s, k_s, v_s, ylru_s, ht_s, cbuf_s = _inproj_lru_sample(
        x_s, state_lru_h[0], buf_tm, row(g_mix[0]), w_in[0], conv_w[0], row(conv_b[0]), w_gates, b_gates,
        *lru_tail, nseq, nt)
    seq_major = lambda a, w: a.reshape(-1, nseq, w).transpose(1, 0, 2)
    q_sm, k_sm, v_sm = seq_major(q_s, SB_W), seq_major(k_s, SB_W), seq_major(v_s, SB_W)
    pages_t = lambda c: c[0].transpose(0, 2, 3, 1).reshape(npool, SB_W, page)
    ysb_s = _sb_sample(page_table, b_sb[0], q_sm, k_sm, v_sm, row(g_out_sb[0]),
                       pages_t(cache_k), pages_t(cache_v))
    x_s2 = x_sample.reshape(n_s, D_MODEL)
    ylru_s2 = seq_major(ylru_s, LRU_W).reshape(n_s, LRU_W)
    ysb_s2 = ysb_s.reshape(n_s, SB_W)

    n_all = n_p + n_s
    router_args = (row(g_ffn[0]), w_r, b_r)
    h8_all = jnp.zeros((n_all * ROW_CHUNKS, LANES), F32)
    x1_p, h8_all, e_p, wsel_p = _outproj_router(
        x_p, ylru_p, ysb_p, w_out[0].astype(BF16), *router_args, TM_OUT, h8_all, 0)
    x1_s, h8_all, e_s, wsel_s = _outproj_router(
        x_s2, ylru_s2, ysb_s2, w_out[0], *router_args, n_s, h8_all, n_p)

    n_tiles = -(-2 * n_all // TM_EXP) + N_EXPERTS
    e_all = jnp.concatenate([e_p[:, :2], e_s[:, :2]], axis=0)
    gsrc, sdst, tile_e, tile_valid, tile_first = _routing_plan(e_all, TM_EXP, n_tiles, n_p, n_s)
    out2 = _experts(tile_e, tile_valid, tile_first, gsrc, sdst, h8_all, w_eg[0], w_eu[0], w_ed[0], TM_EXP,
                    2 * n_all + TM_EXP)

    gfin = row(g_final)
    y_p = _combine_final(x1_p, wsel_p, gfin, out2, TD, 0, n_p)
    y_s = _combine_final(x1_s, wsel_s, gfin, out2, n_s, 2 * n_p, 2 * n_p + n_s)

    heads = lambda a, lead: a.reshape(*lead, HEADS, HEAD_DIM)
    from_t = lambda a: a.reshape(batch, HEADS, HEAD_DIM, seq).transpose(0, 3, 1, 2)[None]
    return (
        y_p.reshape(batch, seq, D_MODEL),
        y_s.reshape(nseq, nt, D_MODEL),
        from_t(kt_p),
        from_t(vt_p),
        ht_p.reshape(1, batch, LRU_W),
        cbuf_p.reshape(1, batch, CONV_W - 1, LRU_W),
        heads(k_sm, (1, nseq, nt)),
        heads(v_sm, (1, nseq, nt)),
        ht_s.reshape(1, nseq, LRU_W),
        cbuf_s.reshape(CONV_W - 1, nseq, LRU_W).transpose(1, 0, 2)[None],
    )
```

```python
import functools
import math

import jax
import jax.numpy as jnp
from jax import lax
from jax.experimental import pallas as pl
from jax.experimental.pallas import tpu as pltpu

F32 = jnp.float32
BF16 = jnp.bfloat16
I32 = jnp.int32

D_MODEL = 1024
LRU_W = 512
SB_W = 512
HEADS = 8
HEAD_DIM = 64
CONV_W = 4
LRU_C = 8.0
EPS = 1e-6
N_GROUPS = 4
EXPERTS_PER_GROUP = 8
N_EXPERTS = 32
D_EXPERT = 512

LANES = 128
SUBLANES = 8
ROW_CHUNKS = D_MODEL // LANES

TM_PROJ = 512
TQ = 256
TM_OUT = 512
TM_EXP = 256
TD = 256
PAGES_PER_STEP = 16
ROUTER_COLS = LANES
VMEM_LIMIT = 56 * 1024 * 1024


def _softplus(z):
    return jnp.maximum(z, 0.0) + jnp.log1p(jnp.exp(-jnp.abs(z)))


def _log_beta_terms(z, accurate):
    t = jnp.exp(-jnp.abs(z))
    log_beta = jnp.minimum(z, 0.0) - (jnp.log1p(t) if accurate else jnp.log(1.0 + t))
    return log_beta - z, log_beta


def _rms(x, g):
    return x * lax.rsqrt(jnp.mean(x * x, axis=-1, keepdims=True) + EPS) * g


def _gelu_tanh(x):
    c = math.sqrt(2.0 / math.pi)
    return x * (0.5 * (1.0 + jnp.tanh(c * (x + 0.044715 * (x * x * x)))))


def _neg_expm1(x):
    u = jnp.exp(x)
    near = x > -1.0
    ratio = x / jnp.where(near & (u < 1.0), jnp.log(u), -1.0)
    return jnp.where(near, jnp.where(u < 1.0, (1.0 - u) * ratio, -x), 1.0 - u)


def _lru_gates(u, gates, lam):
    r = jax.nn.sigmoid(gates[:, :LRU_W])
    i = jax.nn.sigmoid(gates[:, LRU_W:])
    log_a = (-LRU_C) * r * _softplus(-lam)
    a = jnp.exp(log_a)
    xin = jnp.sqrt(_neg_expm1(2.0 * log_a)) * (i * u)
    return a, xin


def _split_bf16(x):
    hi = x.astype(BF16)
    lo = (x - hi.astype(F32)).astype(BF16)
    return hi, lo


def _dot3(x, w):
    x_hi, x_lo = _split_bf16(x)
    w_hi, w_lo = _split_bf16(w)
    n = x.shape[0]
    top = jnp.dot(jnp.concatenate([x_hi, x_lo], axis=0), w_hi, preferred_element_type=F32)
    return top[:n] + top[n:] + jnp.dot(x_hi, w_lo, preferred_element_type=F32)


def _inproj_lru_kernel(x_ref, gmix_ref, win_ref, cw_ref, cb_ref, wg_ref, bg_ref, lam_ref, glru_ref,
                       q_ref, kt_ref, vt_ref, kb_ref, vtb_ref, ylru_ref, ht_ref, cbuf_ref,
                       xlp_ref, a_ref, s_ref, h_ref, *, tm, tk):
    t = pl.program_id(1)
    nt = pl.num_programs(1)

    @pl.when(t == 0)
    def _init():
        xlp_ref[0:SUBLANES, :] = jnp.zeros((SUBLANES, LRU_W), F32)
        h_ref[...] = jnp.zeros_like(h_ref)

    hn = _rms(x_ref[...], gmix_ref[...]).astype(BF16)

    def proj(c):
        return jnp.dot(hn, win_ref[:, c * LRU_W:(c + 1) * LRU_W], preferred_element_type=F32)

    xl = proj(0)
    gl = proj(1)
    q_ref[...] = proj(2).astype(BF16)
    k = proj(3)
    kt_ref[0] = k.T
    kb_ref[...] = k.astype(BF16)
    v_t = proj(4).T
    vt_ref[0] = v_t
    for c in range(tm // tk):
        vtb_ref[0, c] = v_t[:, c * tk:(c + 1) * tk].astype(BF16)

    xlp_ref[SUBLANES:SUBLANES + tm, :] = xl
    cw = cw_ref[...]
    u = cb_ref[...] + cw[3:4, :] * xl
    for j in range(1, CONV_W):
        u = u + cw[3 - j:4 - j, :] * xlp_ref[pl.ds(SUBLANES - j, tm), :]
    xlp_ref[0:SUBLANES, :] = xl[tm - SUBLANES:tm, :]

    gates = jnp.dot(u.astype(BF16), wg_ref[...], preferred_element_type=F32) + bg_ref[...]
    a, xin = _lru_gates(u, gates, lam_ref[...])
    a_ref[...] = a
    s_ref[...] = xin

    row = lax.broadcasted_iota(I32, (SUBLANES, LRU_W), 0)

    def group(g, h):
        r0 = pl.multiple_of(g * SUBLANES, SUBLANES)
        av = a_ref[pl.ds(r0, SUBLANES), :]
        xv = s_ref[pl.ds(r0, SUBLANES), :]
        for d in (1, 2, 4):
            valid = row >= d
            xv = jnp.where(valid, xv + av * pltpu.roll(xv, d, 0), xv)
            av = jnp.where(valid, av * pltpu.roll(av, d, 0), av)
        hv = xv + av * h
        s_ref[pl.ds(r0, SUBLANES), :] = hv
        return hv[SUBLANES - 1:SUBLANES, :]

    h = lax.fori_loop(0, tm // SUBLANES, group, h_ref[0:1, :], unroll=4)
    h_ref[0:1, :] = h

    y = s_ref[...] * _gelu_tanh(gl)
    ylru_ref[...] = _rms(y, glru_ref[...]).astype(BF16)

    @pl.when(t == nt - 1)
    def _fin():
        ht_ref[0] = h
        cbuf_ref[0] = xl[tm - (CONV_W - 1):tm, :]


def _inproj_lru_prompt(x2, g_mix, w_in_bf, conv_w, conv_b, w_gates, b_gates, lam, g_lru, batch, seq):
    tm, tk = TM_PROJ, TQ
    nt = seq // tm
    n = batch * seq
    row_spec = lambda w: pl.BlockSpec((tm, w), lambda b, t: (b * nt + t, 0))
    full = lambda a: pl.BlockSpec(a.shape, lambda b, t: (0,) * a.ndim)
    t_spec = pl.BlockSpec((1, SB_W, tm), lambda b, t: (b, 0, t))
    out_shape = (
        jax.ShapeDtypeStruct((n, SB_W), BF16),
        jax.ShapeDtypeStruct((batch, SB_W, seq), F32),
        jax.ShapeDtypeStruct((batch, SB_W, seq), F32),
        jax.ShapeDtypeStruct((n, SB_W), BF16),
        jax.ShapeDtypeStruct((batch, seq // tk, SB_W, tk), BF16),
        jax.ShapeDtypeStruct((n, LRU_W), BF16),
        jax.ShapeDtypeStruct((batch, 1, LRU_W), F32),
        jax.ShapeDtypeStruct((batch, CONV_W - 1, LRU_W), F32),
    )
    out_specs = (
        row_spec(SB_W), t_spec, t_spec, row_spec(SB_W),
        pl.BlockSpec((1, tm // tk, SB_W, tk), lambda b, t: (b, t, 0, 0)),
        row_spec(LRU_W),
        pl.BlockSpec((1, 1, LRU_W), lambda b, t: (b, 0, 0)),
        pl.BlockSpec((1, CONV_W - 1, LRU_W), lambda b, t: (b, 0, 0)),
    )
    args = (x2, g_mix, w_in_bf, conv_w, conv_b, w_gates, b_gates, lam, g_lru)
    return pl.pallas_call(
        functools.partial(_inproj_lru_kernel, tm=tm, tk=tk),
        grid=(batch, nt),
        in_specs=[row_spec(D_MODEL)] + [full(a) for a in args[1:]],
        out_specs=out_specs,
        out_shape=out_shape,
        scratch_shapes=[
            pltpu.VMEM((tm + SUBLANES, LRU_W), F32),
            pltpu.VMEM((tm, LRU_W), F32),
            pltpu.VMEM((tm, LRU_W), F32),
            pltpu.VMEM((SUBLANES, LRU_W), F32),
        ],
        compiler_params=pltpu.CompilerParams(
            dimension_semantics=("arbitrary", "arbitrary"), vmem_limit_bytes=VMEM_LIMIT),
        name="inproj_lru_prompt",
    )(*args)


def _inproj_lru_sample_kernel(x_ref, h0_ref, buf_ref, gmix_ref, win_ref, cw_ref, cb_ref, wg_ref, bg_ref,
                              lam_ref, glru_ref, q_ref, k_ref, v_ref, ylru_ref, ht_ref, cbuf_ref,
                              *, nseq, nt):
    hn = _rms(x_ref[...], gmix_ref[...])

    def proj(c):
        return _dot3(hn, win_ref[:, c * LRU_W:(c + 1) * LRU_W])

    xl = proj(0)
    gl = proj(1)
    q_ref[...] = proj(2)
    k_ref[...] = proj(3)
    v_ref[...] = proj(4)

    hist = [buf_ref[m * nseq:(m + 1) * nseq, :] for m in range(CONV_W - 1)]
    hist += [xl[t * nseq:(t + 1) * nseq, :] for t in range(nt)]
    cw = cw_ref[...]
    us = []
    for t in range(nt):
        ut = cb_ref[...] + cw[0:1, :] * hist[t]
        for j in range(1, CONV_W):
            ut = ut + cw[j:j + 1, :] * hist[t + j]
        us.append(ut)
    u = jnp.concatenate(us, axis=0)
    gates = _dot3(u, wg_ref[...]) + bg_ref[...]
    a, xin = _lru_gates(u, gates, lam_ref[...])
    h = h0_ref[...]
    hs = []
    for t in range(nt):
        h = a[t * nseq:(t + 1) * nseq, :] * h + xin[t * nseq:(t + 1) * nseq, :]
        hs.append(h)
    y = jnp.concatenate(hs, axis=0) * _gelu_tanh(gl)
    ylru_ref[...] = _rms(y, glru_ref[...])
    ht_ref[...] = h
    cbuf_ref[...] = jnp.concatenate(hist[nt:nt + CONV_W - 1], axis=0)


def _inproj_lru_sample(x_tm, h0, buf_tm, g_mix, w_in, conv_w, conv_b, w_gates, b_gates, lam, g_lru,
                       nseq, nt):
    n = nseq * nt
    out_shape = (
        jax.ShapeDtypeStruct((n, SB_W), F32),
        jax.ShapeDtypeStruct((n, SB_W), F32),
        jax.ShapeDtypeStruct((n, SB_W), F32),
        jax.ShapeDtypeStruct((n, LRU_W), F32),
        jax.ShapeDtypeStruct((nseq, LRU_W), F32),
        jax.ShapeDtypeStruct(((CONV_W - 1) * nseq, LRU_W), F32),
    )
    return pl.pallas_call(
        functools.partial(_inproj_lru_sample_kernel, nseq=nseq, nt=nt),
        out_shape=out_shape,
        compiler_params=pltpu.CompilerParams(vmem_limit_bytes=VMEM_LIMIT),
        name="inproj_lru_sample",
    )(x_tm, h0, buf_tm, g_mix, w_in, conv_w, conv_b, w_gates, b_gates, lam, g_lru)


def _sb_prompt_kernel(bsb_ref, q_ref, k_ref, vt_ref, g_ref, o_ref, acc_ref, c_ref, qm_ref, *, tq):
    i = pl.program_id(1)
    tk = tq
    rowk = lax.broadcasted_iota(I32, (tk, tq), 0)
    colq = lax.broadcasted_iota(I32, (tk, tq), 1)
    diag_mask = rowk < colq
    later_mat = (lax.broadcasted_iota(I32, (tk, tk), 1) > lax.broadcasted_iota(I32, (tk, tk), 0)).astype(BF16)
    lane = lax.broadcasted_iota(I32, (tq, LANES), 1)
    scale = jnp.asarray(HEAD_DIM ** -0.5, BF16)
    heads_per_tile = LANES // HEAD_DIM

    for h in range(HEADS):
        pair, half = divmod(h, heads_per_tile)
        in_head = (lane >= half * HEAD_DIM) & (lane < (half + 1) * HEAD_DIM)
        qm_ref[h] = jnp.where(in_head, q_ref[:, pair * LANES:(pair + 1) * LANES] * scale, jnp.zeros((), BF16))
    acc_ref[...] = jnp.zeros_like(acc_ref)
    c_ref[...] = jnp.zeros_like(c_ref)

    def blocks(js, masked):
        n_pairs = HEADS // heads_per_tile
        kbs = [[k_ref[pl.ds(pl.multiple_of(j * tk, tk), tk), p * LANES:(p + 1) * LANES] for p in range(n_pairs)]
               for j in js]
        units = [(b, h) for b in range(len(js)) for h in range(HEADS)]
        ss = [lax.dot_general(kbs[b][h // heads_per_tile], qm_ref[h], (((1,), (1,)), ((), ())),
                              preferred_element_type=F32) + bsb_ref[h] for b, h in units]
        l1s, lbs = [], []
        for u in range(len(units)):
            l1, lb = _log_beta_terms(ss[u], accurate=False)
            if masked:
                l1 = jnp.where(diag_mask, l1, 0.0)
            l1s.append(l1.astype(BF16))
            lbs.append(lb)
        exs = [jnp.dot(later_mat, l1s[u], preferred_element_type=F32) for u in range(len(units))]
        totals = [exs[u][0:1, :] + l1s[u][0:1, :] for u in range(len(units))]
        ws = []
        for u, (b, h) in enumerate(units):
            c = c_ref[h:h + 1, :]
            if b == 1:
                c = c + totals[h]
            w = jnp.exp(lbs[u] + exs[u] + c)
            if masked:
                w = jnp.where(diag_mask, w, 0.0)
            ws.append(w.astype(BF16))
        for h in range(HEADS):
            rows = slice(h * HEAD_DIM, (h + 1) * HEAD_DIM)
            vts = jnp.concatenate([vt_ref[0, j, rows, :] for j in js], axis=1)
            wcat = jnp.concatenate([ws[b * HEADS + h] for b in range(len(js))], axis=0)
            acc_ref[rows, :] += jnp.dot(vts, wcat, preferred_element_type=F32)
            tot = totals[h] if len(js) == 1 else totals[h] + totals[HEADS + h]
            c_ref[h:h + 1, :] += tot

    blocks([i], True)

    @pl.when(lax.rem(i, 2) == 1)
    def _odd_block():
        blocks([i - 1], False)

    def body(p, carry):
        j = i - lax.rem(i, 2) - 1 - 2 * p
        blocks([j, j - 1], False)
        return carry

    lax.fori_loop(0, i // 2, body, 0)
    o_ref[...] = _rms(acc_ref[...].T, g_ref[...]).astype(BF16)


def _sb_prompt(b_sb, q_bf, k_bf, vt_bf, g_sb, batch, seq):
    tq = TQ
    nq = seq // tq
    n = batch * seq
    return pl.pallas_call(
        functools.partial(_sb_prompt_kernel, tq=tq),
        grid=(batch, nq),
        in_specs=[
            pl.BlockSpec(memory_space=pltpu.SMEM),
            pl.BlockSpec((tq, SB_W), lambda b, i: (b * nq + i, 0)),
            pl.BlockSpec((seq, SB_W), lambda b, i: (b, 0)),
            pl.BlockSpec((1, nq, SB_W, tq), lambda b, i: (b, 0, 0, 0)),
            pl.BlockSpec((1, SB_W), lambda b, i: (0, 0)),
        ],
        out_specs=pl.BlockSpec((tq, SB_W), lambda b, i: (b * nq + i, 0)),
        out_shape=jax.ShapeDtypeStruct((n, SB_W), BF16),
        scratch_shapes=[pltpu.VMEM((SB_W, tq), F32), pltpu.VMEM((HEADS, tq), F32),
                        pltpu.VMEM((HEADS, tq, LANES), BF16)],
        compiler_params=pltpu.CompilerParams(
            dimension_semantics=("arbitrary", "arbitrary"), vmem_limit_bytes=VMEM_LIMIT),
        name="sb_prompt",
    )(b_sb, q_bf, k_bf, vt_bf, g_sb)


def _sb_sample_kernel(pt_ref, bsb_ref, q_ref, kown_ref, vown_ref, g_ref, *rest, npg, nt, page):
    kp = rest[:npg]
    vp = rest[npg:2 * npg]
    o_ref = rest[2 * npg]
    qbd_ref, acc_ref, c_ref = rest[2 * npg + 1:]
    ci = pl.program_id(1)
    nc = pl.num_programs(1)
    rows = nt * HEADS
    rowi = lax.broadcasted_iota(I32, (rows, 1), 0)
    row_h = rowi & (HEADS - 1)
    row_t = rowi >> int(math.log2(HEADS))
    bias = jnp.zeros((rows, 1), F32)
    for h in range(HEADS):
        bias = jnp.where(row_h == h, bsb_ref[h], bias)
    head_of_lane = lax.broadcasted_iota(I32, (HEADS, SB_W), 1) // HEAD_DIM
    head_mask = head_of_lane == lax.broadcasted_iota(I32, (HEADS, SB_W), 0)

    @pl.when(ci == 0)
    def _init():
        q = q_ref[0] * (HEAD_DIM ** -0.5)
        qbd = jnp.concatenate(
            [jnp.where(head_mask, jnp.broadcast_to(q[t:t + 1, :], (HEADS, SB_W)), 0.0) for t in range(nt)],
            axis=0)
        qbd_ref[...] = jnp.concatenate(_split_bf16(qbd), axis=0)
        kown = kown_ref[0]
        vown = vown_ref[0]
        c = jnp.zeros((rows, 1), F32)
        acc = jnp.zeros((rows, SB_W), F32)
        for s in range(nt - 1, -1, -1):
            z = jnp.sum(qbd * kown[s:s + 1, :], axis=-1, keepdims=True) + bias
            vis = s < row_t
            l1, lb = _log_beta_terms(z, accurate=True)
            l1 = jnp.where(vis, l1, 0.0)
            w = jnp.where(vis, jnp.exp(lb + c), 0.0)
            acc = acc + w * vown[s:s + 1, :]
            c = c + l1
        acc_ref[...] = acc
        c_ref[...] = jnp.broadcast_to(c, (rows, LANES))

    later_mat = (lax.broadcasted_iota(I32, (page, page), 0) > lax.broadcasted_iota(I32, (page, page), 1)).astype(BF16)
    nt_dims = (((1,), (1,)), ((), ()))
    q2 = qbd_ref[...]
    q_hi = q2[:rows]

    def fold(x):
        return x[:rows] + x[rows:]

    ss = []
    for i in range(npg):
        kt_hi, kt_lo = _split_bf16(kp[i][0])
        ss.append(fold(jnp.dot(q2, kt_hi, preferred_element_type=F32))
                  + jnp.dot(q_hi, kt_lo, preferred_element_type=F32) + bias)
    l1s, lbs, exs = [], [], []
    for i in range(npg):
        l1, lb = _log_beta_terms(ss[i], accurate=True)
        l1s.append(l1)
        lbs.append(lb)
        exs.append(fold(jnp.dot(jnp.concatenate(_split_bf16(l1), axis=0), later_mat,
                                preferred_element_type=F32)))
    c = c_ref[...]
    acc = acc_ref[...]
    for i in range(npg):
        w = jnp.exp(lbs[i] + exs[i] + c)
        w_hi, w_lo = _split_bf16(w)
        vt_hi, vt_lo = _split_bf16(vp[i][0])
        acc = (acc + fold(lax.dot_general(jnp.concatenate([w_hi, w_lo], axis=0), vt_hi, nt_dims,
                                          preferred_element_type=F32))
               + lax.dot_general(w_hi, vt_lo, nt_dims, preferred_element_type=F32))
        c = c + jnp.broadcast_to(exs[i][:, 0:1] + l1s[i][:, 0:1], (rows, LANES))
    acc_ref[...] = acc
    c_ref[...] = c

    @pl.when(ci == nc - 1)
    def _fin():
        ys = [jnp.sum(jnp.where(head_mask, acc[t * HEADS:(t + 1) * HEADS, :], 0.0), axis=0, keepdims=True)
              for t in range(nt)]
        o_ref[0] = _rms(jnp.concatenate(ys, axis=0), g_ref[...])


def _sb_sample(page_table, b_sb, q_s, k_own, v_own, g_sb, cache_kt, cache_vt):
    nseq, npages = page_table.shape
    nt = q_s.shape[1]
    page = cache_kt.shape[2]
    npg = PAGES_PER_STEP
    nc = npages // npg
    rows = nt * HEADS

    def page_spec(i):
        return pl.BlockSpec(
            (1, SB_W, page),
            lambda s, c, pt, i=i: (pt[s * npages + (npages - 1) - (c * npg + i)], 0, 0))

    seq_spec = pl.BlockSpec((1, nt, SB_W), lambda s, c, pt: (s, 0, 0))
    grid_spec = pltpu.PrefetchScalarGridSpec(
        num_scalar_prefetch=1,
        grid=(nseq, nc),
        in_specs=[pl.BlockSpec(memory_space=pltpu.SMEM), seq_spec, seq_spec, seq_spec,
                  pl.BlockSpec((1, SB_W), lambda s, c, pt: (0, 0))]
                 + [page_spec(i) for i in range(npg)] + [page_spec(i) for i in range(npg)],
        out_specs=pl.BlockSpec((1, nt, SB_W), lambda s, c, pt: (s, 0, 0)),
        scratch_shapes=[pltpu.VMEM((2 * rows, SB_W), BF16), pltpu.VMEM((rows, SB_W), F32),
                        pltpu.VMEM((rows, LANES), F32)],
    )
    return pl.pallas_call(
        functools.partial(_sb_sample_kernel, npg=npg, nt=nt, page=page),
        grid_spec=grid_spec,
        out_shape=jax.ShapeDtypeStruct((nseq, nt, SB_W), F32),
        compiler_params=pltpu.CompilerParams(
            dimension_semantics=("arbitrary", "arbitrary"), vmem_limit_bytes=VMEM_LIMIT),
        name="sb_sample",
    )(page_table.reshape(-1), b_sb, q_s, k_own, v_own, g_sb, *([cache_kt] * npg), *([cache_vt] * npg))


def _outproj_router_kernel(x_ref, yl_ref, ys_ref, wout_ref, gffn_ref, wr_ref, br_ref, *rest, tm, precise):
    x1_ref, h8_ref, e_ref, w_ref = rest[-4:]
    lru_rows = slice(0, LRU_W)
    sb_rows = slice(LRU_W, LRU_W + SB_W)
    if precise:
        x1 = (x_ref[...] + _dot3(yl_ref[...], wout_ref[lru_rows, :]) + _dot3(ys_ref[...], wout_ref[sb_rows, :]))
    else:
        x1 = (x_ref[...]
              + jnp.dot(yl_ref[...], wout_ref[lru_rows, :], preferred_element_type=F32)
              + jnp.dot(ys_ref[...], wout_ref[sb_rows, :], preferred_element_type=F32))
    x1_ref[...] = x1
    h = _rms(x1, gffn_ref[...])
    for j in range(ROW_CHUNKS):
        h8_ref[pl.ds(j, tm, stride=ROW_CHUNKS), :] = h[:, j * LANES:(j + 1) * LANES]

    logits = _dot3(h, wr_ref[...]) + br_ref[...]

    lane = lax.broadcasted_iota(I32, (tm, ROUTER_COLS), 1)
    big = jnp.asarray(ROUTER_COLS, I32)
    neg = jnp.asarray(-jnp.inf, F32)
    gl = jnp.where(lane < N_GROUPS, logits, neg)
    gmax = jnp.max(gl, axis=-1, keepdims=True)
    g_idx = jnp.min(jnp.where(gl == gmax, lane, big), axis=-1, keepdims=True)
    p_g = 1.0 / jnp.sum(jnp.exp(gl - gmax), axis=-1, keepdims=True)
    e_lo = N_GROUPS + g_idx * EXPERTS_PER_GROUP
    el = jnp.where((lane >= e_lo) & (lane < e_lo + EXPERTS_PER_GROUP), logits, neg)
    m1 = jnp.max(el, axis=-1, keepdims=True)
    i1 = jnp.min(jnp.where(el == m1, lane, big), axis=-1, keepdims=True)
    el2 = jnp.where(lane == i1, neg, el)
    m2 = jnp.max(el2, axis=-1, keepdims=True)
    i2 = jnp.min(jnp.where(el2 == m2, lane, big), axis=-1, keepdims=True)
    r = jnp.exp(m2 - m1)
    w1 = p_g / (1.0 + r)
    w2 = w1 * r
    e_ref[...] = jnp.where(lane == 0, i1 - N_GROUPS, jnp.where(lane == 1, i2 - N_GROUPS, 0))
    w_ref[...] = jnp.where(lane == 0, w1, jnp.where(lane == 1, w2, 0.0))


def _outproj_router(x2, ylru, ysb, w_out, g_ffn, w_r, b_r, tm, h8_buf, row0):
    n = x2.shape[0]
    n_all = h8_buf.shape[0] // ROW_CHUNKS
    precise = ylru.dtype == F32
    assert ysb.dtype == ylru.dtype == w_out.dtype and row0 % tm == 0
    blk0 = row0 // tm
    row_spec = lambda w: pl.BlockSpec((tm, w), lambda t: (t, 0))
    full = lambda a: pl.BlockSpec(a.shape, lambda t: (0,) * a.ndim)
    args = [x2, ylru, ysb, w_out, g_ffn, w_r, b_r, h8_buf]
    in_specs = [row_spec(D_MODEL), row_spec(LRU_W), row_spec(SB_W), full(w_out), full(g_ffn), full(w_r),
                full(b_r), pl.BlockSpec(memory_space=pl.ANY)]
    aliases = {len(args) - 1: 1}
    return pl.pallas_call(
        functools.partial(_outproj_router_kernel, tm=tm, precise=precise),
        grid=(n // tm,),
        in_specs=in_specs,
        out_specs=(row_spec(D_MODEL), pl.BlockSpec((tm * ROW_CHUNKS, LANES), lambda t: (blk0 + t, 0)),
                   row_spec(ROUTER_COLS), row_spec(ROUTER_COLS)),
        out_shape=(jax.ShapeDtypeStruct((n, D_MODEL), F32),
                   jax.ShapeDtypeStruct((n_all * ROW_CHUNKS, LANES), F32),
                   jax.ShapeDtypeStruct((n, ROUTER_COLS), I32),
                   jax.ShapeDtypeStruct((n, ROUTER_COLS), F32)),
        input_output_aliases=aliases,
        compiler_params=pltpu.CompilerParams(
            dimension_semantics=("arbitrary",), vmem_limit_bytes=VMEM_LIMIT),
        name="outproj_router",
    )(*args)


def _row_tile(ref, row):
    return ref.at[pl.ds(pl.multiple_of(row * ROW_CHUNKS, ROW_CHUNKS), ROW_CHUNKS)]


def _experts_kernel(te_ref, tv_ref, tf_ref, gsrc_ref, gnext_ref, sdst_ref, h8_hbm, wg_ref, wu_ref, wd_ref,
                    out_hbm, xbuf_ref, obuf_ref, x2_ref, wgb_ref, wub_ref, wdb_ref, gsem, ssem, *, tm):
    del te_ref
    t = pl.program_id(0)
    nt = pl.num_programs(0)
    slot = lax.rem(t, 2)
    other = 1 - slot

    def gather(idx_ref, s):
        return lambda r: pltpu.make_async_copy(
            _row_tile(h8_hbm, idx_ref[0, 0, r]), _row_tile(xbuf_ref.at[s], r), gsem.at[s])

    def scatter(s):
        return lambda r: pltpu.make_async_copy(
            _row_tile(obuf_ref.at[s], r), _row_tile(out_hbm, sdst_ref[0, 0, r]), ssem.at[s])

    def start_all(copy):
        def body(r, c):
            copy(r).start()
            return c
        lax.fori_loop(0, tm, body, 0, unroll=8)

    tile_rows = pl.ds(0, tm * ROW_CHUNKS)

    def wait_gather(s):
        pltpu.make_async_copy(h8_hbm.at[tile_rows], xbuf_ref.at[s], gsem.at[s]).wait()

    def wait_scatter(s):
        pltpu.make_async_copy(obuf_ref.at[s], out_hbm.at[tile_rows], ssem.at[s]).wait()

    @pl.when(t == 0)
    def _first_gather():
        start_all(gather(gsrc_ref, slot))

    @pl.when(t + 1 < nt)
    def _next_gather():
        start_all(gather(gnext_ref, other))

    wait_gather(slot)

    @pl.when(tf_ref[t] == 1)
    def _cast_weights():
        wgb_ref[...] = wg_ref[0].astype(BF16)
        wub_ref[...] = wu_ref[0].astype(BF16)
        wdb_ref[...] = wd_ref[0].astype(BF16)

    @pl.when(tv_ref[t] == 1)
    def _compute():
        for j in range(ROW_CHUNKS):
            x2_ref[:, j * LANES:(j + 1) * LANES] = xbuf_ref[slot, pl.ds(j, tm, stride=ROW_CHUNKS), :].astype(BF16)
        x = x2_ref[...]
        g = jnp.dot(x, wgb_ref[...], preferred_element_type=F32)
        u = jnp.dot(x, wub_ref[...], preferred_element_type=F32)
        act = (g * jax.nn.sigmoid(g) * u).astype(BF16)
        y = jnp.dot(act, wdb_ref[...], preferred_element_type=F32)
        for j in range(ROW_CHUNKS):
            obuf_ref[slot, pl.ds(j, tm, stride=ROW_CHUNKS), :] = y[:, j * LANES:(j + 1) * LANES]

    @pl.when(tv_ref[t] == 0)
    def _pad():
        obuf_ref[slot] = jnp.zeros(obuf_ref.shape[1:], F32)

    @pl.when(t >= 1)
    def _prev_scatter_done():
        wait_scatter(other)

    start_all(scatter(slot))

    @pl.when(t == nt - 1)
    def _last_scatter_done():
        wait_scatter(slot)


def _experts(tile_e, tile_valid, tile_first, gsrc, sdst, h8_all, w_eg, w_eu, w_ed, tm, out_rows):
    n_tiles = tile_e.shape[0]
    idx_block = (1, 1, tm)
    grid_spec = pltpu.PrefetchScalarGridSpec(
        num_scalar_prefetch=3,
        grid=(n_tiles,),
        in_specs=[
            pl.BlockSpec(idx_block, lambda t, te, tv, tf: (t, 0, 0), memory_space=pltpu.SMEM),
            pl.BlockSpec(idx_block, lambda t, te, tv, tf: (jnp.minimum(t + 1, n_tiles - 1), 0, 0),
                         memory_space=pltpu.SMEM),
            pl.BlockSpec(idx_block, lambda t, te, tv, tf: (t, 0, 0), memory_space=pltpu.SMEM),
            pl.BlockSpec(memory_space=pl.ANY),
            pl.BlockSpec((1, D_MODEL, D_EXPERT), lambda t, te, tv, tf: (te[t], 0, 0)),
            pl.BlockSpec((1, D_MODEL, D_EXPERT), lambda t, te, tv, tf: (te[t], 0, 0)),
            pl.BlockSpec((1, D_EXPERT, D_MODEL), lambda t, te, tv, tf: (te[t], 0, 0)),
        ],
        out_specs=pl.BlockSpec(memory_space=pl.ANY),
        scratch_shapes=[pltpu.VMEM((2, tm * ROW_CHUNKS, LANES), F32), pltpu.VMEM((2, tm * ROW_CHUNKS, LANES), F32),
                        pltpu.VMEM((tm, D_MODEL), BF16), pltpu.VMEM((D_MODEL, D_EXPERT), BF16),
                        pltpu.VMEM((D_MODEL, D_EXPERT), BF16), pltpu.VMEM((D_EXPERT, D_MODEL), BF16),
                        pltpu.SemaphoreType.DMA((2,)), pltpu.SemaphoreType.DMA((2,))],
    )
    return pl.pallas_call(
        functools.partial(_experts_kernel, tm=tm),
        grid_spec=grid_spec,
        out_shape=jax.ShapeDtypeStruct((out_rows * ROW_CHUNKS, LANES), F32),
        compiler_params=pltpu.CompilerParams(
            dimension_semantics=("arbitrary",), vmem_limit_bytes=VMEM_LIMIT),
        name="experts",
    )(tile_e, tile_valid, tile_first, gsrc, gsrc, sdst, h8_all, w_eg, w_eu, w_ed)


def _combine_kernel(x1_ref, wsel_ref, gfin_ref, y0_ref, y1_ref, o_ref, *, td):
    w = wsel_ref[...]
    w0 = w[:, 0:1]
    w1 = w[:, 1:2]
    cols = []
    for j in range(ROW_CHUNKS):
        cols.append(x1_ref[:, j * LANES:(j + 1) * LANES]
                    + w0 * y0_ref[pl.ds(j, td, stride=ROW_CHUNKS), :]
                    + w1 * y1_ref[pl.ds(j, td, stride=ROW_CHUNKS), :])
    o_ref[...] = _rms(jnp.concatenate(cols, axis=1), gfin_ref[...])


def _combine_final(x1, wsel, g_final, out2, td, row_slot0, row_slot1):
    n = x1.shape[0]
    assert row_slot0 % td == 0 and row_slot1 % td == 0
    blk0, blk1 = row_slot0 // td, row_slot1 // td
    tile_spec = lambda b0: pl.BlockSpec((td * ROW_CHUNKS, LANES), lambda t: (b0 + t, 0))
    return pl.pallas_call(
        functools.partial(_combine_kernel, td=td),
        grid=(n // td,),
        in_specs=[pl.BlockSpec((td, D_MODEL), lambda t: (t, 0)),
                  pl.BlockSpec((td, ROUTER_COLS), lambda t: (t, 0)),
                  pl.BlockSpec((1, D_MODEL), lambda t: (0, 0)),
                  tile_spec(blk0), tile_spec(blk1)],
        out_specs=pl.BlockSpec((td, D_MODEL), lambda t: (t, 0)),
        out_shape=jax.ShapeDtypeStruct((n, D_MODEL), F32),
        compiler_params=pltpu.CompilerParams(
            dimension_semantics=("arbitrary",), vmem_limit_bytes=VMEM_LIMIT),
        name="combine_final",
    )(x1, wsel, g_final, out2, out2)


def _routing_plan(e_sel, tm, n_tiles, n_p, n_s):
    n2 = e_sel.size
    e_flat = e_sel.reshape(-1)
    order = jnp.argsort(e_flat, stable=True).astype(I32)
    experts = jnp.arange(N_EXPERTS, dtype=I32)
    counts = jnp.sum((e_flat[:, None] == experts[None, :]).astype(I32), axis=0)
    tiles_per = (counts + tm - 1) // tm
    tile_end = jnp.cumsum(tiles_per)
    tile_start = tile_end - tiles_per
    packed_start = jnp.cumsum(counts) - counts
    total = tile_end[-1]
    t = jnp.arange(n_tiles, dtype=I32)
    tile_e = jnp.minimum(jnp.sum((t[:, None] >= tile_end[None, :]).astype(I32), axis=1), N_EXPERTS - 1)
    last_e = jnp.max(jnp.where(counts > 0, experts, 0))
    valid = t < total
    tile_e = jnp.where(valid, tile_e, last_e).astype(I32)
    first = (valid & (t == tile_start[tile_e])).astype(I32)

    p = jnp.arange(n_tiles * tm, dtype=I32)
    e_p = jnp.minimum(jnp.sum((p[:, None] >= (tile_end * tm)[None, :]).astype(I32), axis=1), N_EXPERTS - 1)
    is_e = e_p[:, None] == experts[None, :]
    pick = lambda table: jnp.sum(jnp.where(is_e, table[None, :], 0), axis=1)
    local = p - pick(tile_start) * tm
    row_valid = (p < total * tm) & (local < pick(counts))
    f = order[jnp.clip(pick(packed_start) + local, 0, n2 - 1)]
    token, slot = f >> 1, f & 1
    gsrc = jnp.where(row_valid, token, 0).astype(I32)
    dst = jnp.where(token < n_p, slot * n_p + token, 2 * n_p + slot * n_s + (token - n_p))
    sdst = jnp.where(row_valid, dst, 2 * (n_p + n_s) + (p & (tm - 1))).astype(I32)
    shape3 = (n_tiles, 1, tm)
    return gsrc.reshape(shape3), sdst.reshape(shape3), tile_e, valid.astype(I32), first


def _block_diag(w):
    nb, d, _ = w.shape
    eye = jnp.eye(nb, dtype=w.dtype)
    return (eye[:, None, :, None] * w[:, :, None, :]).reshape(nb * d, nb * d)


def kernel(x_prompt, x_sample, cache_k, cache_v, state_lru_h, state_conv, page_table, g_mix, w_in, conv_w, conv_b, w_a, b_a, w_i, b_i, lam, b_sb, g_out_lru, g_out_sb, w_out, g_ffn, w_rg, b_rg, w_re, b_re, w_eg, w_eu, w_ed, g_final):
    depth = w_in.shape[0]
    assert depth == 1, "single-layer trunk"
    batch, seq, _ = x_prompt.shape
    nseq, nt, _ = x_sample.shape
    npool, page = cache_k.shape[1], cache_k.shape[2]
    n_p = batch * seq
    n_s = nseq * nt
    row = lambda a: a.reshape(1, -1)

    w_gates = jnp.concatenate([_block_diag(w_a[0]), _block_diag(w_i[0])], axis=1)
    b_gates = jnp.concatenate([b_a[0], b_i[0]]).reshape(1, -1)
    w_r = jnp.zeros((D_MODEL, ROUTER_COLS), F32)
    w_r = w_r.at[:, :N_GROUPS].set(w_rg[0]).at[:, N_GROUPS:N_GROUPS + N_EXPERTS].set(w_re[0])
    b_r = jnp.zeros((1, ROUTER_COLS), F32)
    b_r = b_r.at[0, :N_GROUPS].set(b_rg[0]).at[0, N_GROUPS:N_GROUPS + N_EXPERTS].set(b_re[0])
    lru_tail = (row(lam[0]), row(g_out_lru[0]))

    x_p = x_prompt.reshape(n_p, D_MODEL)
    q_p, kt_p, vt_p, kb_p, vtb_p, ylru_p, ht_p, cbuf_p = _inproj_lru_prompt(
        x_p, row(g_mix[0]), w_in[0].astype(BF16), conv_w[0], row(conv_b[0]), w_gates.astype(BF16), b_gates,
        *lru_tail, batch, seq)
    ysb_p = _sb_prompt(b_sb[0], q_p, kb_p, vtb_p, row(g_out_sb[0]), batch, seq)

    x_s = x_sample.transpose(1, 0, 2).reshape(n_s, D_MODEL)
    buf_tm = state_conv[0].transpose(1, 0, 2).reshape((CONV_W - 1) * nseq, LRU_W)
    q_s, k_s, v_s, ylru_s, ht_s, cbuf_s = _inproj_lru_sample(
        x_s, state_lru_h[0], buf_tm, row(g_mix[0]), w_in[0], conv_w[0], row(conv_b[0]), w_gates, b_gates,
        *lru_tail, nseq, nt)
    seq_major = lambda a, w: a.reshape(-1, nseq, w).transpose(1, 0, 2)
    q_sm, k_sm, v_sm = seq_major(q_s, SB_W), seq_major(k_s, SB_W), seq_major(v_s, SB_W)
    pages_t = lambda c: c[0].transpose(0, 2, 3, 1).reshape(npool, SB_W, page)
    ysb_s = _sb_sample(page_table, b_sb[0], q_sm, k_sm, v_sm, row(g_out_sb[0]),
                       pages_t(cache_k), pages_t(cache_v))
    x_s2 = x_sample.reshape(n_s, D_MODEL)
    ylru_s2 = seq_major(ylru_s, LRU_W).reshape(n_s, LRU_W)
    ysb_s2 = ysb_s.reshape(n_s, SB_W)

    n_all = n_p + n_s
    router_args = (row(g_ffn[0]), w_r, b_r)
    h8_all = jnp.zeros((n_all * ROW_CHUNKS, LANES), F32)
    x1_p, h8_all, e_p, wsel_p = _outproj_router(
        x_p, ylru_p, ysb_p, w_out[0].astype(BF16), *router_args, TM_OUT, h8_all, 0)
    x1_s, h8_all, e_s, wsel_s = _outproj_router(
        x_s2, ylru_s2, ysb_s2, w_out[0], *router_args, n_s, h8_all, n_p)

    n_tiles = -(-2 * n_all // TM_EXP) + N_EXPERTS
    e_all = jnp.concatenate([e_p[:, :2], e_s[:, :2]], axis=0)
    gsrc, sdst, tile_e, tile_valid, tile_first = _routing_plan(e_all, TM_EXP, n_tiles, n_p, n_s)
    out2 = _experts(tile_e, tile_valid, tile_first, gsrc, sdst, h8_all, w_eg[0], w_eu[0], w_ed[0], TM_EXP,
                    2 * n_all + TM_EXP)

    gfin = row(g_final)
    y_p = _combine_final(x1_p, wsel_p, gfin, out2, TD, 0, n_p)
    y_s = _combine_final(x1_s, wsel_s, gfin, out2, n_s, 2 * n_p, 2 * n_p + n_s)

    heads = lambda a, lead: a.reshape(*lead, HEADS, HEAD_DIM)
    from_t = lambda a: a.reshape(batch, HEADS, HEAD_DIM, seq).transpose(0, 3, 1, 2)[None]
    return (
        y_p.reshape(batch, seq, D_MODEL),
        y_s.reshape(nseq, nt, D_MODEL),
        from_t(kt_p),
        from_t(vt_p),
        ht_p.reshape(1, batch, LRU_W),
        cbuf_p.reshape(1, batch, CONV_W - 1, LRU_W),
        heads(k_sm, (1, nseq, nt)),
        heads(v_sm, (1, nseq, nt)),
        ht_s.reshape(1, nseq, LRU_W),
        cbuf_s.reshape(CONV_W - 1, nseq, LRU_W).transpose(1, 0, 2)[None],
    )
```

```python
import functools
import math

import jax
import jax.numpy as jnp
from jax import lax
from jax.experimental import pallas as pl
from jax.experimental.pallas import tpu as pltpu

F32 = jnp.float32
BF16 = jnp.bfloat16
I32 = jnp.int32

D_MODEL = 1024
LRU_W = 512
SB_W = 512
HEADS = 8
HEAD_DIM = 64
CONV_W = 4
LRU_C = 8.0
EPS = 1e-6
N_GROUPS = 4
EXPERTS_PER_GROUP = 8
N_EXPERTS = 32
D_EXPERT = 512

LANES = 128
SUBLANES = 8
ROW_CHUNKS = D_MODEL // LANES

TM_PROJ = 512
TQ = 256
TM_OUT = 512
TM_EXP = 256
TD = 256
PAGES_PER_STEP = 16
ROUTER_COLS = LANES
VMEM_LIMIT = 56 * 1024 * 1024


def _softplus(z):
    return jnp.maximum(z, 0.0) + jnp.log1p(jnp.exp(-jnp.abs(z)))


def _log_beta_terms(z, accurate):
    t = jnp.exp(-jnp.abs(z))
    log_beta = jnp.minimum(z, 0.0) - (jnp.log1p(t) if accurate else jnp.log(1.0 + t))
    return log_beta - z, log_beta


def _rms(x, g):
    return x * lax.rsqrt(jnp.mean(x * x, axis=-1, keepdims=True) + EPS) * g


def _gelu_tanh(x):
    c = math.sqrt(2.0 / math.pi)
    return x * (0.5 * (1.0 + jnp.tanh(c * (x + 0.044715 * (x * x * x)))))


def _neg_expm1(x):
    u = jnp.exp(x)
    near = x > -1.0
    ratio = x / jnp.where(near & (u < 1.0), jnp.log(u), -1.0)
    return jnp.where(near, jnp.where(u < 1.0, (1.0 - u) * ratio, -x), 1.0 - u)


def _lru_gates(u, gates, lam):
    r = jax.nn.sigmoid(gates[:, :LRU_W])
    i = jax.nn.sigmoid(gates[:, LRU_W:])
    log_a = (-LRU_C) * r * _softplus(-lam)
    a = jnp.exp(log_a)
    xin = jnp.sqrt(_neg_expm1(2.0 * log_a)) * (i * u)
    return a, xin


def _split_bf16(x):
    hi = x.astype(BF16)
    lo = (x - hi.astype(F32)).astype(BF16)
    return hi, lo


def _dot3(x, w):
    x_hi, x_lo = _split_bf16(x)
    w_hi, w_lo = _split_bf16(w)
    n = x.shape[0]
    top = jnp.dot(jnp.concatenate([x_hi, x_lo], axis=0), w_hi, preferred_element_type=F32)
    return top[:n] + top[n:] + jnp.dot(x_hi, w_lo, preferred_element_type=F32)


def _inproj_lru_kernel(x_ref, gmix_ref, win_ref, cw_ref, cb_ref, wg_ref, bg_ref, lam_ref, glru_ref,
                       q_ref, kt_ref, vt_ref, kb_ref, vtb_ref, ylru_ref, ht_ref, cbuf_ref,
                       xlp_ref, a_ref, s_ref, h_ref, *, tm, tk):
    t = pl.program_id(1)
    nt = pl.num_programs(1)

    @pl.when(t == 0)
    def _init():
        xlp_ref[0:SUBLANES, :] = jnp.zeros((SUBLANES, LRU_W), F32)
        h_ref[...] = jnp.zeros_like(h_ref)

    hn = _rms(x_ref[...], gmix_ref[...]).astype(BF16)

    def proj(c):
        return jnp.dot(hn, win_ref[:, c * LRU_W:(c + 1) * LRU_W], preferred_element_type=F32)

    xl = proj(0)
    gl = proj(1)
    q_ref[...] = proj(2).astype(BF16)
    k = proj(3)
    kt_ref[0] = k.T
    kb_ref[...] = k.astype(BF16)
    v_t = proj(4).T
    vt_ref[0] = v_t
    for c in range(tm // tk):
        vtb_ref[0, c] = v_t[:, c * tk:(c + 1) * tk].astype(BF16)

    xlp_ref[SUBLANES:SUBLANES + tm, :] = xl
    cw = cw_ref[...]
    u = cb_ref[...] + cw[3:4, :] * xl
    for j in range(1, CONV_W):
        u = u + cw[3 - j:4 - j, :] * xlp_ref[pl.ds(SUBLANES - j, tm), :]
    xlp_ref[0:SUBLANES, :] = xl[tm - SUBLANES:tm, :]

    gates = jnp.dot(u.astype(BF16), wg_ref[...], preferred_element_type=F32) + bg_ref[...]
    a, xin = _lru_gates(u, gates, lam_ref[...])
    a_ref[...] = a
    s_ref[...] = xin

    row = lax.broadcasted_iota(I32, (SUBLANES, LRU_W), 0)

    def group(g, h):
        r0 = pl.multiple_of(g * SUBLANES, SUBLANES)
        av = a_ref[pl.ds(r0, SUBLANES), :]
        xv = s_ref[pl.ds(r0, SUBLANES), :]
        for d in (1, 2, 4):
            valid = row >= d
            xv = jnp.where(valid, xv + av * pltpu.roll(xv, d, 0), xv)
            av = jnp.where(valid, av * pltpu.roll(av, d, 0), av)
        hv = xv + av * h
        s_ref[pl.ds(r0, SUBLANES), :] = hv
        return hv[SUBLANES - 1:SUBLANES, :]

    h = lax.fori_loop(0, tm // SUBLANES, group, h_ref[0:1, :], unroll=4)
    h_ref[0:1, :] = h

    y = s_ref[...] * _gelu_tanh(gl)
    ylru_ref[...] = _rms(y, glru_ref[...]).astype(BF16)

    @pl.when(t == nt - 1)
    def _fin():
        ht_ref[0] = h
        cbuf_ref[0] = xl[tm - (CONV_W - 1):tm, :]


def _inproj_lru_prompt(x2, g_mix, w_in_bf, conv_w, conv_b, w_gates, b_gates, lam, g_lru, batch, seq):
    tm, tk = TM_PROJ, TQ
    nt = seq // tm
    n = batch * seq
    row_spec = lambda w: pl.BlockSpec((tm, w), lambda b, t: (b * nt + t, 0))
    full = lambda a: pl.BlockSpec(a.shape, lambda b, t: (0,) * a.ndim)
    t_spec = pl.BlockSpec((1, SB_W, tm), lambda b, t: (b, 0, t))
    out_shape = (
        jax.ShapeDtypeStruct((n, SB_W), BF16),
        jax.ShapeDtypeStruct((batch, SB_W, seq), F32),
        jax.ShapeDtypeStruct((batch, SB_W, seq), F32),
        jax.ShapeDtypeStruct((n, SB_W), BF16),
        jax.ShapeDtypeStruct((batch, seq // tk, SB_W, tk), BF16),
        jax.ShapeDtypeStruct((n, LRU_W), BF16),
        jax.ShapeDtypeStruct((batch, 1, LRU_W), F32),
        jax.ShapeDtypeStruct((batch, CONV_W - 1, LRU_W), F32),
    )
    out_specs = (
        row_spec(SB_W), t_spec, t_spec, row_spec(SB_W),
        pl.BlockSpec((1, tm // tk, SB_W, tk), lambda b, t: (b, t, 0, 0)),
        row_spec(LRU_W),
        pl.BlockSpec((1, 1, LRU_W), lambda b, t: (b, 0, 0)),
        pl.BlockSpec((1, CONV_W - 1, LRU_W), lambda b, t: (b, 0, 0)),
    )
    args = (x2, g_mix, w_in_bf, conv_w, conv_b, w_gates, b_gates, lam, g_lru)
    return pl.pallas_call(
        functools.partial(_inproj_lru_kernel, tm=tm, tk=tk),
        grid=(batch, nt),
        in_specs=[row_spec(D_MODEL)] + [full(a) for a in args[1:]],
        out_specs=out_specs,
        out_shape=out_shape,
        scratch_shapes=[
            pltpu.VMEM((tm + SUBLANES, LRU_W), F32),
            pltpu.VMEM((tm, LRU_W), F32),
            pltpu.VMEM((tm, LRU_W), F32),
            pltpu.VMEM((SUBLANES, LRU_W), F32),
        ],
        compiler_params=pltpu.CompilerParams(
            dimension_semantics=("arbitrary", "arbitrary"), vmem_limit_bytes=VMEM_LIMIT),
        name="inproj_lru_prompt",
    )(*args)


def _inproj_lru_sample_kernel(x_ref, h0_ref, buf_ref, gmix_ref, win_ref, cw_ref, cb_ref, wg_ref, bg_ref,
                              lam_ref, glru_ref, q_ref, k_ref, v_ref, ylru_ref, ht_ref, cbuf_ref,
                              *, nseq, nt):
    hn = _rms(x_ref[...], gmix_ref[...])

    def proj(c):
        return _dot3(hn, win_ref[:, c * LRU_W:(c + 1) * LRU_W])

    xl = proj(0)
    gl = proj(1)
    q_ref[...] = proj(2)
    k_ref[...] = proj(3)
    v_ref[...] = proj(4)

    hist = [buf_ref[m * nseq:(m + 1) * nseq, :] for m in range(CONV_W - 1)]
    hist += [xl[t * nseq:(t + 1) * nseq, :] for t in range(nt)]
    cw = cw_ref[...]
    us = []
    for t in range(nt):
        ut = cb_ref[...] + cw[0:1, :] * hist[t]
        for j in range(1, CONV_W):
            ut = ut + cw[j:j + 1, :] * hist[t + j]
        us.append(ut)
    u = jnp.concatenate(us, axis=0)
    gates = _dot3(u, wg_ref[...]) + bg_ref[...]
    a, xin = _lru_gates(u, gates, lam_ref[...])
    h = h0_ref[...]
    hs = []
    for t in range(nt):
        h = a[t * nseq:(t + 1) * nseq, :] * h + xin[t * nseq:(t + 1) * nseq, :]
        hs.append(h)
    y = jnp.concatenate(hs, axis=0) * _gelu_tanh(gl)
    ylru_ref[...] = _rms(y, glru_ref[...])
    ht_ref[...] = h
    cbuf_ref[...] = jnp.concatenate(hist[nt:nt + CONV_W - 1], axis=0)


def _inproj_lru_sample(x_tm, h0, buf_tm, g_mix, w_in, conv_w, conv_b, w_gates, b_gates, lam, g_lru,
                       nseq, nt):
    n = nseq * nt
    out_shape = (
        jax.ShapeDtypeStruct((n, SB_W), F32),
        jax.ShapeDtypeStruct((n, SB_W), F32),
        jax.ShapeDtypeStruct((n, SB_W), F32),
        jax.ShapeDtypeStruct((n, LRU_W), F32),
        jax.ShapeDtypeStruct((nseq, LRU_W), F32),
        jax.ShapeDtypeStruct(((CONV_W - 1) * nseq, LRU_W), F32),
    )
    return pl.pallas_call(
        functools.partial(_inproj_lru_sample_kernel, nseq=nseq, nt=nt),
        out_shape=out_shape,
        compiler_params=pltpu.CompilerParams(vmem_limit_bytes=VMEM_LIMIT),
        name="inproj_lru_sample",
    )(x_tm, h0, buf_tm, g_mix, w_in, conv_w, conv_b, w_gates, b_gates, lam, g_lru)


def _sb_prompt_kernel(bsb_ref, q_ref, k_ref, vt_ref, g_ref, o_ref, acc_ref, c_ref, qm_ref, *, tq):
    i = pl.program_id(1)
    tk = tq
    rowk = lax.broadcasted_iota(I32, (tk, tq), 0)
    colq = lax.broadcasted_iota(I32, (tk, tq), 1)
    diag_mask = rowk < colq
    later_mat = (lax.broadcasted_iota(I32, (tk, tk), 1) > lax.broadcasted_iota(I32, (tk, tk), 0)).astype(BF16)
    lane = lax.broadcasted_iota(I32, (tq, LANES), 1)
    scale = jnp.asarray(HEAD_DIM ** -0.5, BF16)
    heads_per_tile = LANES // HEAD_DIM

    for h in range(HEADS):
        pair, half = divmod(h, heads_per_tile)
        in_head = (lane >= half * HEAD_DIM) & (lane < (half + 1) * HEAD_DIM)
        qm_ref[h] = jnp.where(in_head, q_ref[:, pair * LANES:(pair + 1) * LANES] * scale, jnp.zeros((), BF16))
    acc_ref[...] = jnp.zeros_like(acc_ref)
    c_ref[...] = jnp.zeros_like(c_ref)

    def blocks(js, masked):
        n_pairs = HEADS // heads_per_tile
        kbs = [[k_ref[pl.ds(pl.multiple_of(j * tk, tk), tk), p * LANES:(p + 1) * LANES] for p in range(n_pairs)]
               for j in js]
        units = [(b, h) for b in range(len(js)) for h in range(HEADS)]
        ss = [lax.dot_general(kbs[b][h // heads_per_tile], qm_ref[h], (((1,), (1,)), ((), ())),
                              preferred_element_type=F32) + bsb_ref[h] for b, h in units]
        l1s, lbs = [], []
        for u in range(len(units)):
            l1, lb = _log_beta_terms(ss[u], accurate=False)
            if masked:
                l1 = jnp.where(diag_mask, l1, 0.0)
            l1s.append(l1.astype(BF16))
            lbs.append(lb)
        exs = [jnp.dot(later_mat, l1s[u], preferred_element_type=F32) for u in range(len(units))]
        totals = [exs[u][0:1, :] + l1s[u][0:1, :] for u in range(len(units))]
        ws = []
        for u, (b, h) in enumerate(units):
            c = c_ref[h:h + 1, :]
            if b == 1:
                c = c + totals[h]
            w = jnp.exp(lbs[u] + exs[u] + c)
            if masked:
                w = jnp.where(diag_mask, w, 0.0)
            ws.append(w.astype(BF16))
        for h in range(HEADS):
            rows = slice(h * HEAD_DIM, (h + 1) * HEAD_DIM)
            vts = jnp.concatenate([vt_ref[0, j, rows, :] for j in js], axis=1)
            wcat = jnp.concatenate([ws[b * HEADS + h] for b in range(len(js))], axis=0)
            acc_ref[rows, :] += jnp.dot(vts, wcat, preferred_element_type=F32)
            tot = totals[h] if len(js) == 1 else totals[h] + totals[HEADS + h]
            c_ref[h:h + 1, :] += tot

    blocks([i], True)

    @pl.when(lax.rem(i, 2) == 1)
    def _odd_block():
        blocks([i - 1], False)

    def body(p, carry):
        j = i - lax.rem(i, 2) - 1 - 2 * p
        blocks([j, j - 1], False)
        return carry

    lax.fori_loop(0, i // 2, body, 0)
    o_ref[...] = _rms(acc_ref[...].T, g_ref[...]).astype(BF16)


def _sb_prompt(b_sb, q_bf, k_bf, vt_bf, g_sb, batch, seq):
    tq = TQ
    nq = seq // tq
    n = batch * seq
    return pl.pallas_call(
        functools.partial(_sb_prompt_kernel, tq=tq),
        grid=(batch, nq),
        in_specs=[
            pl.BlockSpec(memory_space=pltpu.SMEM),
            pl.BlockSpec((tq, SB_W), lambda b, i: (b * nq + i, 0)),
            pl.BlockSpec((seq, SB_W), lambda b, i: (b, 0)),
            pl.BlockSpec((1, nq, SB_W, tq), lambda b, i: (b, 0, 0, 0)),
            pl.BlockSpec((1, SB_W), lambda b, i: (0, 0)),
        ],
        out_specs=pl.BlockSpec((tq, SB_W), lambda b, i: (b * nq + i, 0)),
        out_shape=jax.ShapeDtypeStruct((n, SB_W), BF16),
        scratch_shapes=[pltpu.VMEM((SB_W, tq), F32), pltpu.VMEM((HEADS, tq), F32),
                        pltpu.VMEM((HEADS, tq, LANES), BF16)],
        compiler_params=pltpu.CompilerParams(
            dimension_semantics=("arbitrary", "arbitrary"), vmem_limit_bytes=VMEM_LIMIT),
        name="sb_prompt",
    )(b_sb, q_bf, k_bf, vt_bf, g_sb)


def _sb_sample_kernel(pt_ref, bsb_ref, q_ref, kown_ref, vown_ref, g_ref, *rest, npg, nt, page):
    kp = rest[:npg]
    vp = rest[npg:2 * npg]
    o_ref = rest[2 * npg]
    qbd_ref, acc_ref, c_ref = rest[2 * npg + 1:]
    ci = pl.program_id(1)
    nc = pl.num_programs(1)
    rows = nt * HEADS
    rowi = lax.broadcasted_iota(I32, (rows, 1), 0)
    row_h = rowi & (HEADS - 1)
    row_t = rowi >> int(math.log2(HEADS))
    bias = jnp.zeros((rows, 1), F32)
    for h in range(HEADS):
        bias = jnp.where(row_h == h, bsb_ref[h], bias)
    head_of_lane = lax.broadcasted_iota(I32, (HEADS, SB_W), 1) // HEAD_DIM
    head_mask = head_of_lane == lax.broadcasted_iota(I32, (HEADS, SB_W), 0)

    @pl.when(ci == 0)
    def _init():
        q = q_ref[0] * (HEAD_DIM ** -0.5)
        qbd = jnp.concatenate(
            [jnp.where(head_mask, jnp.broadcast_to(q[t:t + 1, :], (HEADS, SB_W)), 0.0) for t in range(nt)],
            axis=0)
        qbd_ref[...] = jnp.concatenate(_split_bf16(qbd), axis=0)
        kown = kown_ref[0]
        vown = vown_ref[0]
        c = jnp.zeros((rows, 1), F32)
        acc = jnp.zeros((rows, SB_W), F32)
        for s in range(nt - 1, -1, -1):
            z = jnp.sum(qbd * kown[s:s + 1, :], axis=-1, keepdims=True) + bias
            vis = s < row_t
            l1, lb = _log_beta_terms(z, accurate=True)
            l1 = jnp.where(vis, l1, 0.0)
            w = jnp.where(vis, jnp.exp(lb + c), 0.0)
            acc = acc + w * vown[s:s + 1, :]
            c = c + l1
        acc_ref[...] = acc
        c_ref[...] = jnp.broadcast_to(c, (rows, LANES))

    later_mat = (lax.broadcasted_iota(I32, (page, page), 0) > lax.broadcasted_iota(I32, (page, page), 1)).astype(BF16)
    nt_dims = (((1,), (1,)), ((), ()))
    q2 = qbd_ref[...]
    q_hi = q2[:rows]

    def fold(x):
        return x[:rows] + x[rows:]

    ss = []
    for i in range(npg):
        kt_hi, kt_lo = _split_bf16(kp[i][0])
        ss.append(fold(jnp.dot(q2, kt_hi, preferred_element_type=F32))
                  + jnp.dot(q_hi, kt_lo, preferred_element_type=F32) + bias)
    l1s, lbs, exs = [], [], []
    for i in range(npg):
        l1, lb = _log_beta_terms(ss[i], accurate=True)
        l1s.append(l1)
        lbs.append(lb)
        exs.append(fold(jnp.dot(jnp.concatenate(_split_bf16(l1), axis=0), later_mat,
                                preferred_element_type=F32)))
    c = c_ref[...]
    acc = acc_ref[...]
    for i in range(npg):
        w = jnp.exp(lbs[i] + exs[i] + c)
        w_hi, w_lo = _split_bf16(w)
        vt_hi, vt_lo = _split_bf16(vp[i][0])
        acc = (acc + fold(lax.dot_general(jnp.concatenate([w_hi, w_lo], axis=0), vt_hi, nt_dims,
                                          preferred_element_type=F32))
               + lax.dot_general(w_hi, vt_lo, nt_dims, preferred_element_type=F32))
        c = c + jnp.broadcast_to(exs[i][:, 0:1] + l1s[i][:, 0:1], (rows, LANES))
    acc_ref[...] = acc
    c_ref[...] = c

    @pl.when(ci == nc - 1)
    def _fin():
        ys = [jnp.sum(jnp.where(head_mask, acc[t * HEADS:(t + 1) * HEADS, :], 0.0), axis=0, keepdims=True)
              for t in range(nt)]
        o_ref[0] = _rms(jnp.concatenate(ys, axis=0), g_ref[...])


def _sb_sample(page_table, b_sb, q_s, k_own, v_own, g_sb, cache_kt, cache_vt):
    nseq, npages = page_table.shape
    nt = q_s.shape[1]
    page = cache_kt.shape[2]
    npg = PAGES_PER_STEP
    nc = npages // npg
    rows = nt * HEADS

    def page_spec(i):
        return pl.BlockSpec(
            (1, SB_W, page),
            lambda s, c, pt, i=i: (pt[s * npages + (npages - 1) - (c * npg + i)], 0, 0))

    seq_spec = pl.BlockSpec((1, nt, SB_W), lambda s, c, pt: (s, 0, 0))
    grid_spec = pltpu.PrefetchScalarGridSpec(
        num_scalar_prefetch=1,
        grid=(nseq, nc),
        in_specs=[pl.BlockSpec(memory_space=pltpu.SMEM), seq_spec, seq_spec, seq_spec,
                  pl.BlockSpec((1, SB_W), lambda s, c, pt: (0, 0))]
                 + [page_spec(i) for i in range(npg)] + [page_spec(i) for i in range(npg)],
        out_specs=pl.BlockSpec((1, nt, SB_W), lambda s, c, pt: (s, 0, 0)),
        scratch_shapes=[pltpu.VMEM((2 * rows, SB_W), BF16), pltpu.VMEM((rows, SB_W), F32),
                        pltpu.VMEM((rows, LANES), F32)],
    )
    return pl.pallas_call(
        functools.partial(_sb_sample_kernel, npg=npg, nt=nt, page=page),
        grid_spec=grid_spec,
        out_shape=jax.ShapeDtypeStruct((nseq, nt, SB_W), F32),
        compiler_params=pltpu.CompilerParams(
            dimension_semantics=("arbitrary", "arbitrary"), vmem_limit_bytes=VMEM_LIMIT),
        name="sb_sample",
    )(page_table.reshape(-1), b_sb, q_s, k_own, v_own, g_sb, *([cache_kt] * npg), *([cache_vt] * npg))


def _outproj_router_kernel(x_ref, yl_ref, ys_ref, wout_ref, gffn_ref, wr_ref, br_ref,
                           x1_ref, h8_ref, e_ref, w_ref, *, tm, precise):
    lru_rows = slice(0, LRU_W)
    sb_rows = slice(LRU_W, LRU_W + SB_W)
    if precise:
        x1 = (x_ref[...] + _dot3(yl_ref[...], wout_ref[lru_rows, :]) + _dot3(ys_ref[...], wout_ref[sb_rows, :]))
    else:
        x1 = (x_ref[...]
              + jnp.dot(yl_ref[...], wout_ref[lru_rows, :], preferred_element_type=F32)
              + jnp.dot(ys_ref[...], wout_ref[sb_rows, :], preferred_element_type=F32))
    x1_ref[...] = x1
    h = _rms(x1, gffn_ref[...])
    for j in range(ROW_CHUNKS):
        h8_ref[pl.ds(j, tm, stride=ROW_CHUNKS), :] = h[:, j * LANES:(j + 1) * LANES]

    logits = _dot3(h, wr_ref[...]) + br_ref[...]

    lane = lax.broadcasted_iota(I32, (tm, ROUTER_COLS), 1)
    big = jnp.asarray(ROUTER_COLS, I32)
    neg = jnp.asarray(-jnp.inf, F32)
    gl = jnp.where(lane < N_GROUPS, logits, neg)
    gmax = jnp.max(gl, axis=-1, keepdims=True)
    g_idx = jnp.min(jnp.where(gl == gmax, lane, big), axis=-1, keepdims=True)
    p_g = 1.0 / jnp.sum(jnp.exp(gl - gmax), axis=-1, keepdims=True)
    e_lo = N_GROUPS + g_idx * EXPERTS_PER_GROUP
    el = jnp.where((lane >= e_lo) & (lane < e_lo + EXPERTS_PER_GROUP), logits, neg)
    m1 = jnp.max(el, axis=-1, keepdims=True)
    i1 = jnp.min(jnp.where(el == m1, lane, big), axis=-1, keepdims=True)
    el2 = jnp.where(lane == i1, neg, el)
    m2 = jnp.max(el2, axis=-1, keepdims=True)
    i2 = jnp.min(jnp.where(el2 == m2, lane, big), axis=-1, keepdims=True)
    r = jnp.exp(m2 - m1)
    w1 = p_g / (1.0 + r)
    w2 = w1 * r
    e_ref[...] = jnp.where(lane == 0, i1 - N_GROUPS, jnp.where(lane == 1, i2 - N_GROUPS, 0))
    w_ref[...] = jnp.where(lane == 0, w1, jnp.where(lane == 1, w2, 0.0))


def _outproj_router(x2, ylru, ysb, w_out, g_ffn, w_r, b_r, tm):
    n = x2.shape[0]
    precise = ylru.dtype == F32
    assert ysb.dtype == ylru.dtype == w_out.dtype
    row_spec = lambda w: pl.BlockSpec((tm, w), lambda t: (t, 0))
    full = lambda a: pl.BlockSpec(a.shape, lambda t: (0,) * a.ndim)
    return pl.pallas_call(
        functools.partial(_outproj_router_kernel, tm=tm, precise=precise),
        grid=(n // tm,),
        in_specs=[row_spec(D_MODEL), row_spec(LRU_W), row_spec(SB_W), full(w_out), full(g_ffn), full(w_r),
                  full(b_r)],
        out_specs=(row_spec(D_MODEL), pl.BlockSpec((tm * ROW_CHUNKS, LANES), lambda t: (t, 0)),
                   row_spec(ROUTER_COLS), row_spec(ROUTER_COLS)),
        out_shape=(jax.ShapeDtypeStruct((n, D_MODEL), F32),
                   jax.ShapeDtypeStruct((n * ROW_CHUNKS, LANES), F32),
                   jax.ShapeDtypeStruct((n, ROUTER_COLS), I32),
                   jax.ShapeDtypeStruct((n, ROUTER_COLS), F32)),
        compiler_params=pltpu.CompilerParams(
            dimension_semantics=("arbitrary",), vmem_limit_bytes=VMEM_LIMIT),
        name="outproj_router",
    )(x2, ylru, ysb, w_out, g_ffn, w_r, b_r)


def _row_tile(ref, row):
    return ref.at[pl.ds(pl.multiple_of(row * ROW_CHUNKS, ROW_CHUNKS), ROW_CHUNKS)]


def _for_each_row(td, fn):
    def body(r8, c):
        for k in range(SUBLANES):
            fn(r8 * SUBLANES + k)
        return c
    lax.fori_loop(0, td // SUBLANES, body, 0)


def _dispatch_kernel(pos_ref, h8_ref, xs_in_hbm, xs_hbm, stage_ref, sem, *, td):
    del xs_in_hbm
    t = pl.program_id(0)
    nt = pl.num_programs(0)
    slot = lax.rem(t, 2)

    def wait_slot(s):
        for _ in range(2):
            pltpu.make_async_copy(stage_ref.at[s], xs_hbm.at[pl.ds(0, td * ROW_CHUNKS)], sem.at[s]).wait()

    @pl.when(t >= 2)
    def _slot_free():
        wait_slot(slot)

    stage_ref[slot] = h8_ref[...]

    def start(r):
        for k in range(2):
            pltpu.make_async_copy(_row_tile(stage_ref.at[slot], r), _row_tile(xs_hbm, pos_ref[0, 0, 2 * r + k]),
                                  sem.at[slot]).start()

    _for_each_row(td, start)

    @pl.when(t == nt - 1)
    def _drain():
        @pl.when(t >= 1)
        def _():
            wait_slot(1 - slot)
        wait_slot(slot)


def _dispatch(pos, h8, xs8, td):
    n = pos.shape[0]
    pos3 = pos.reshape(n // td, 1, 2 * td)
    return pl.pallas_call(
        functools.partial(_dispatch_kernel, td=td),
        grid=(n // td,),
        in_specs=[pl.BlockSpec((1, 1, 2 * td), lambda t: (t, 0, 0), memory_space=pltpu.SMEM),
                  pl.BlockSpec((td * ROW_CHUNKS, LANES), lambda t: (t, 0)),
                  pl.BlockSpec(memory_space=pl.ANY)],
        out_specs=pl.BlockSpec(memory_space=pl.ANY),
        out_shape=jax.ShapeDtypeStruct(xs8.shape, xs8.dtype),
        scratch_shapes=[pltpu.VMEM((2, td * ROW_CHUNKS, LANES), F32), pltpu.SemaphoreType.DMA((2,))],
        input_output_aliases={2: 0},
        compiler_params=pltpu.CompilerParams(
            dimension_semantics=("arbitrary",), vmem_limit_bytes=VMEM_LIMIT),
        name="dispatch",
    )(pos3, h8, xs8)


def _experts_kernel(te_ref, tv_ref, tf_ref, xs_ref, wg_ref, wu_ref, wd_ref, ys_ref,
                    x2_ref, wgb_ref, wub_ref, wdb_ref, *, tm):
    del te_ref
    t = pl.program_id(0)

    @pl.when(tf_ref[t] == 1)
    def _cast_weights():
        wgb_ref[...] = wg_ref[0].astype(BF16)
        wub_ref[...] = wu_ref[0].astype(BF16)
        wdb_ref[...] = wd_ref[0].astype(BF16)

    @pl.when(tv_ref[t] == 1)
    def _compute():
        for j in range(ROW_CHUNKS):
            x2_ref[:, j * LANES:(j + 1) * LANES] = xs_ref[pl.ds(j, tm, stride=ROW_CHUNKS), :].astype(BF16)
        x = x2_ref[...]
        g = jnp.dot(x, wgb_ref[...], preferred_element_type=F32)
        u = jnp.dot(x, wub_ref[...], preferred_element_type=F32)
        act = (g * jax.nn.sigmoid(g) * u).astype(BF16)
        y = jnp.dot(act, wdb_ref[...], preferred_element_type=F32)
        for j in range(ROW_CHUNKS):
            ys_ref[pl.ds(j, tm, stride=ROW_CHUNKS), :] = y[:, j * LANES:(j + 1) * LANES]

    @pl.when(tv_ref[t] == 0)
    def _pad():
        ys_ref[...] = jnp.zeros_like(ys_ref)


def _experts(tile_e, tile_valid, tile_first, xs8, w_eg, w_eu, w_ed, tm):
    n_tiles = tile_e.shape[0]
    grid_spec = pltpu.PrefetchScalarGridSpec(
        num_scalar_prefetch=3,
        grid=(n_tiles,),
        in_specs=[
            pl.BlockSpec((tm * ROW_CHUNKS, LANES), lambda t, te, tv, tf: (t, 0)),
            pl.BlockSpec((1, D_MODEL, D_EXPERT), lambda t, te, tv, tf: (te[t], 0, 0)),
            pl.BlockSpec((1, D_MODEL, D_EXPERT), lambda t, te, tv, tf: (te[t], 0, 0)),
            pl.BlockSpec((1, D_EXPERT, D_MODEL), lambda t, te, tv, tf: (te[t], 0, 0)),
        ],
        out_specs=pl.BlockSpec((tm * ROW_CHUNKS, LANES), lambda t, te, tv, tf: (t, 0)),
        scratch_shapes=[pltpu.VMEM((tm, D_MODEL), BF16), pltpu.VMEM((D_MODEL, D_EXPERT), BF16),
                        pltpu.VMEM((D_MODEL, D_EXPERT), BF16), pltpu.VMEM((D_EXPERT, D_MODEL), BF16)],
    )
    return pl.pallas_call(
        functools.partial(_experts_kernel, tm=tm),
        grid_spec=grid_spec,
        out_shape=jax.ShapeDtypeStruct(xs8.shape, F32),
        compiler_params=pltpu.CompilerParams(
            dimension_semantics=("arbitrary",), vmem_limit_bytes=VMEM_LIMIT),
        name="experts",
    )(tile_e, tile_valid, tile_first, xs8, w_eg, w_eu, w_ed)


def _combine_kernel(pos_ref, posn_ref, x1_ref, wsel_ref, gfin_ref, ys_hbm, o_ref, buf_ref, sem, *, td):
    t = pl.program_id(0)
    nt = pl.num_programs(0)
    slot = lax.rem(t, 2)

    def start_gather(idx_ref, s):
        def start(r):
            for k in range(2):
                pltpu.make_async_copy(_row_tile(ys_hbm, idx_ref[0, 0, 2 * r + k]), _row_tile(buf_ref.at[s, k], r),
                                      sem.at[s]).start()
        _for_each_row(td, start)

    @pl.when(t == 0)
    def _first():
        start_gather(pos_ref, slot)

    @pl.when(t + 1 < nt)
    def _next():
        start_gather(posn_ref, 1 - slot)

    for k in range(2):
        pltpu.make_async_copy(ys_hbm.at[pl.ds(0, td * ROW_CHUNKS)], buf_ref.at[slot, k], sem.at[slot]).wait()

    w = wsel_ref[...]
    w0 = w[:, 0:1]
    w1 = w[:, 1:2]
    cols = []
    for j in range(ROW_CHUNKS):
        cols.append(x1_ref[:, j * LANES:(j + 1) * LANES]
                    + w0 * buf_ref[slot, 0, pl.ds(j, td, stride=ROW_CHUNKS), :]
                    + w1 * buf_ref[slot, 1, pl.ds(j, td, stride=ROW_CHUNKS), :])
    o_ref[...] = _rms(jnp.concatenate(cols, axis=1), gfin_ref[...])


def _combine_final(pos, x1, wsel, g_final, ys8, td):
    n = pos.shape[0]
    nt = n // td
    pos3 = pos.reshape(nt, 1, 2 * td)
    idx_spec = lambda fn: pl.BlockSpec((1, 1, 2 * td), fn, memory_space=pltpu.SMEM)
    return pl.pallas_call(
        functools.partial(_combine_kernel, td=td),
        grid=(nt,),
        in_specs=[idx_spec(lambda t: (t, 0, 0)), idx_spec(lambda t: (jnp.minimum(t + 1, nt - 1), 0, 0)),
                  pl.BlockSpec((td, D_MODEL), lambda t: (t, 0)),
                  pl.BlockSpec((td, ROUTER_COLS), lambda t: (t, 0)),
                  pl.BlockSpec((1, D_MODEL), lambda t: (0, 0)),
                  pl.BlockSpec(memory_space=pl.ANY)],
        out_specs=pl.BlockSpec((td, D_MODEL), lambda t: (t, 0)),
        out_shape=jax.ShapeDtypeStruct((n, D_MODEL), F32),
        scratch_shapes=[pltpu.VMEM((2, 2, td * ROW_CHUNKS, LANES), F32), pltpu.SemaphoreType.DMA((2,))],
        compiler_params=pltpu.CompilerParams(
            dimension_semantics=("arbitrary",), vmem_limit_bytes=VMEM_LIMIT),
        name="combine_final",
    )(pos3, pos3, x1, wsel, g_final, ys8)


def _routing_plan(e_sel, tm, n_tiles):
    e_flat = e_sel.reshape(-1)
    onehot = (e_flat[:, None] == jnp.arange(N_EXPERTS, dtype=I32)[None, :]).astype(I32)
    csum = jnp.cumsum(onehot, axis=0)
    rank = jnp.sum(onehot * csum, axis=1) - 1
    counts = csum[-1]
    tiles_per = (counts + tm - 1) // tm
    tile_end = jnp.cumsum(tiles_per)
    tile_start = tile_end - tiles_per
    pos = (tile_start * tm)[e_flat] + rank
    total = tile_end[-1]
    t = jnp.arange(n_tiles, dtype=I32)
    tile_e = jnp.minimum(jnp.sum((t[:, None] >= tile_end[None, :]).astype(I32), axis=1), N_EXPERTS - 1)
    last_e = jnp.max(jnp.where(counts > 0, jnp.arange(N_EXPERTS, dtype=I32), 0))
    valid = t < total
    tile_e = jnp.where(valid, tile_e, last_e).astype(I32)
    first = (valid & (t == tile_start[tile_e])).astype(I32)
    return pos.reshape(-1, 2).astype(I32), tile_e, valid.astype(I32), first


def _block_diag(w):
    nb, d, _ = w.shape
    eye = jnp.eye(nb, dtype=w.dtype)
    return (eye[:, None, :, None] * w[:, :, None, :]).reshape(nb * d, nb * d)


def kernel(x_prompt, x_sample, cache_k, cache_v, state_lru_h, state_conv, page_table, g_mix, w_in, conv_w, conv_b, w_a, b_a, w_i, b_i, lam, b_sb, g_out_lru, g_out_sb, w_out, g_ffn, w_rg, b_rg, w_re, b_re, w_eg, w_eu, w_ed, g_final):
    depth = w_in.shape[0]
    assert depth == 1, "single-layer trunk"
    batch, seq, _ = x_prompt.shape
    nseq, nt, _ = x_sample.shape
    npool, page = cache_k.shape[1], cache_k.shape[2]
    n_p = batch * seq
    n_s = nseq * nt
    row = lambda a: a.reshape(1, -1)

    w_gates = jnp.concatenate([_block_diag(w_a[0]), _block_diag(w_i[0])], axis=1)
    b_gates = jnp.concatenate([b_a[0], b_i[0]]).reshape(1, -1)
    w_r = jnp.zeros((D_MODEL, ROUTER_COLS), F32)
    w_r = w_r.at[:, :N_GROUPS].set(w_rg[0]).at[:, N_GROUPS:N_GROUPS + N_EXPERTS].set(w_re[0])
    b_r = jnp.zeros((1, ROUTER_COLS), F32)
    b_r = b_r.at[0, :N_GROUPS].set(b_rg[0]).at[0, N_GROUPS:N_GROUPS + N_EXPERTS].set(b_re[0])
    lru_tail = (row(lam[0]), row(g_out_lru[0]))

    x_p = x_prompt.reshape(n_p, D_MODEL)
    q_p, kt_p, vt_p, kb_p, vtb_p, ylru_p, ht_p, cbuf_p = _inproj_lru_prompt(
        x_p, row(g_mix[0]), w_in[0].astype(BF16), conv_w[0], row(conv_b[0]), w_gates.astype(BF16), b_gates,
        *lru_tail, batch, seq)
    ysb_p = _sb_prompt(b_sb[0], q_p, kb_p, vtb_p, row(g_out_sb[0]), batch, seq)

    x_s = x_sample.transpose(1, 0, 2).reshape(n_s, D_MODEL)
    buf_tm = state_conv[0].transpose(1, 0, 2).reshape((CONV_W - 1) * nseq, LRU_W)
    q_s, k_s, v_s, ylru_s, ht_s, cbuf_s = _inproj_lru_sample(
        x_s, state_lru_h[0], buf_tm, row(g_mix[0]), w_in[0], conv_w[0], row(conv_b[0]), w_gates, b_gates,
        *lru_tail, nseq, nt)
    seq_major = lambda a, w: a.reshape(-1, nseq, w).transpose(1, 0, 2)
    q_sm, k_sm, v_sm = seq_major(q_s, SB_W), seq_major(k_s, SB_W), seq_major(v_s, SB_W)
    pages_t = lambda c: c[0].transpose(0, 2, 3, 1).reshape(npool, SB_W, page)
    ysb_s = _sb_sample(page_table, b_sb[0], q_sm, k_sm, v_sm, row(g_out_sb[0]),
                       pages_t(cache_k), pages_t(cache_v))
    x_s2 = x_sample.reshape(n_s, D_MODEL)
    ylru_s2 = seq_major(ylru_s, LRU_W).reshape(n_s, LRU_W)
    ysb_s2 = ysb_s.reshape(n_s, SB_W)

    router_args = (row(g_ffn[0]), w_r, b_r)
    x1_p, h8_p, e_p, wsel_p = _outproj_router(x_p, ylru_p, ysb_p, w_out[0].astype(BF16), *router_args, TM_OUT)
    x1_s, h8_s, e_s, wsel_s = _outproj_router(x_s2, ylru_s2, ysb_s2, w_out[0], *router_args, n_s)

    n_all = n_p + n_s
    n_tiles = -(-2 * n_all // TM_EXP) + N_EXPERTS
    e_all = jnp.concatenate([e_p[:, :2], e_s[:, :2]], axis=0)
    pos, tile_e, tile_valid, tile_first = _routing_plan(e_all, TM_EXP, n_tiles)
    pos_p, pos_s = pos[:n_p], pos[n_p:]

    xs8 = jnp.zeros((n_tiles * TM_EXP * ROW_CHUNKS, LANES), F32)
    xs8 = _dispatch(pos_p, h8_p, xs8, TD)
    xs8 = _dispatch(pos_s, h8_s, xs8, n_s)
    ys8 = _experts(tile_e, tile_valid, tile_first, xs8, w_eg[0], w_eu[0], w_ed[0], TM_EXP)

    gfin = row(g_final)
    y_p = _combine_final(pos_p, x1_p, wsel_p, gfin, ys8, TD)
    y_s = _combine_final(pos_s, x1_s, wsel_s, gfin, ys8, n_s)

    heads = lambda a, lead: a.reshape(*lead, HEADS, HEAD_DIM)
    from_t = lambda a: a.reshape(batch, HEADS, HEAD_DIM, seq).transpose(0, 3, 1, 2)[None]
    return (
        y_p.reshape(batch, seq, D_MODEL),
        y_s.reshape(nseq, nt, D_MODEL),
        from_t(kt_p),
        from_t(vt_p),
        ht_p.reshape(1, batch, LRU_W),
        cbuf_p.reshape(1, batch, CONV_W - 1, LRU_W),
        heads(k_sm, (1, nseq, nt)),
        heads(v_sm, (1, nseq, nt)),
        ht_s.reshape(1, nseq, LRU_W),
        cbuf_s.reshape(CONV_W - 1, nseq, LRU_W).transpose(1, 0, 2)[None],
    )
```

```python
import functools
import math

import jax
import jax.numpy as jnp
from jax import lax
from jax.experimental import pallas as pl
from jax.experimental.pallas import tpu as pltpu

F32 = jnp.float32
BF16 = jnp.bfloat16
I32 = jnp.int32

D_MODEL = 1024
LRU_W = 512
SB_W = 512
HEADS = 8
HEAD_DIM = 64
CONV_W = 4
LRU_C = 8.0
EPS = 1e-6
N_GROUPS = 4
EXPERTS_PER_GROUP = 8
N_EXPERTS = 32
D_EXPERT = 512

LANES = 128
SUBLANES = 8
ROW_CHUNKS = D_MODEL // LANES

TM_PROJ = 512
TQ = 256
TM_OUT = 512
TM_EXP = 256
TD = 512
PAGES_PER_STEP = 16
ROUTER_COLS = LANES
VMEM_LIMIT = 56 * 1024 * 1024


def _softplus(z):
    return jnp.maximum(z, 0.0) + jnp.log1p(jnp.exp(-jnp.abs(z)))


def _log_beta_terms(z, accurate):
    t = jnp.exp(-jnp.abs(z))
    log_beta = jnp.minimum(z, 0.0) - (jnp.log1p(t) if accurate else jnp.log(1.0 + t))
    return log_beta - z, log_beta


def _rms(x, g):
    return x * lax.rsqrt(jnp.mean(x * x, axis=-1, keepdims=True) + EPS) * g


def _gelu_tanh(x):
    c = math.sqrt(2.0 / math.pi)
    return x * (0.5 * (1.0 + jnp.tanh(c * (x + 0.044715 * (x * x * x)))))


def _neg_expm1(x):
    u = jnp.exp(x)
    near = x > -1.0
    ratio = x / jnp.where(near & (u < 1.0), jnp.log(u), -1.0)
    return jnp.where(near, jnp.where(u < 1.0, (1.0 - u) * ratio, -x), 1.0 - u)


def _lru_gates(u, gates, lam):
    r = jax.nn.sigmoid(gates[:, :LRU_W])
    i = jax.nn.sigmoid(gates[:, LRU_W:])
    log_a = (-LRU_C) * r * _softplus(-lam)
    a = jnp.exp(log_a)
    xin = jnp.sqrt(_neg_expm1(2.0 * log_a)) * (i * u)
    return a, xin


def _split_bf16(x):
    hi = x.astype(BF16)
    lo = (x - hi.astype(F32)).astype(BF16)
    return hi, lo


def _dot3(x, w):
    x_hi, x_lo = _split_bf16(x)
    w_hi, w_lo = _split_bf16(w)
    n = x.shape[0]
    top = jnp.dot(jnp.concatenate([x_hi, x_lo], axis=0), w_hi, preferred_element_type=F32)
    return top[:n] + top[n:] + jnp.dot(x_hi, w_lo, preferred_element_type=F32)


def _inproj_lru_kernel(x_ref, gmix_ref, win_ref, cw_ref, cb_ref, wg_ref, bg_ref, lam_ref, glru_ref,
                       q_ref, kt_ref, vt_ref, kb_ref, vtb_ref, ylru_ref, ht_ref, cbuf_ref,
                       xlp_ref, a_ref, s_ref, h_ref, *, tm, tk):
    t = pl.program_id(1)
    nt = pl.num_programs(1)

    @pl.when(t == 0)
    def _init():
        xlp_ref[0:SUBLANES, :] = jnp.zeros((SUBLANES, LRU_W), F32)
        h_ref[...] = jnp.zeros_like(h_ref)

    hn = _rms(x_ref[...], gmix_ref[...]).astype(BF16)

    def proj(c):
        return jnp.dot(hn, win_ref[:, c * LRU_W:(c + 1) * LRU_W], preferred_element_type=F32)

    xl = proj(0)
    gl = proj(1)
    q_ref[...] = proj(2).astype(BF16)
    k = proj(3)
    kt_ref[0] = k.T
    kb_ref[...] = k.astype(BF16)
    v_t = proj(4).T
    vt_ref[0] = v_t
    for c in range(tm // tk):
        vtb_ref[0, c] = v_t[:, c * tk:(c + 1) * tk].astype(BF16)

    xlp_ref[SUBLANES:SUBLANES + tm, :] = xl
    cw = cw_ref[...]
    u = cb_ref[...] + cw[3:4, :] * xl
    for j in range(1, CONV_W):
        u = u + cw[3 - j:4 - j, :] * xlp_ref[pl.ds(SUBLANES - j, tm), :]
    xlp_ref[0:SUBLANES, :] = xl[tm - SUBLANES:tm, :]

    gates = jnp.dot(u.astype(BF16), wg_ref[...], preferred_element_type=F32) + bg_ref[...]
    a, xin = _lru_gates(u, gates, lam_ref[...])
    a_ref[...] = a
    s_ref[...] = xin

    row = lax.broadcasted_iota(I32, (SUBLANES, LRU_W), 0)

    def group(g, h):
        r0 = pl.multiple_of(g * SUBLANES, SUBLANES)
        av = a_ref[pl.ds(r0, SUBLANES), :]
        xv = s_ref[pl.ds(r0, SUBLANES), :]
        for d in (1, 2, 4):
            valid = row >= d
            xv = jnp.where(valid, xv + av * pltpu.roll(xv, d, 0), xv)
            av = jnp.where(valid, av * pltpu.roll(av, d, 0), av)
        hv = xv + av * h
        s_ref[pl.ds(r0, SUBLANES), :] = hv
        return hv[SUBLANES - 1:SUBLANES, :]

    h = lax.fori_loop(0, tm // SUBLANES, group, h_ref[0:1, :], unroll=4)
    h_ref[0:1, :] = h

    y = s_ref[...] * _gelu_tanh(gl)
    ylru_ref[...] = _rms(y, glru_ref[...]).astype(BF16)

    @pl.when(t == nt - 1)
    def _fin():
        ht_ref[0] = h
        cbuf_ref[0] = xl[tm - (CONV_W - 1):tm, :]


def _inproj_lru_prompt(x2, g_mix, w_in_bf, conv_w, conv_b, w_gates, b_gates, lam, g_lru, batch, seq):
    tm, tk = TM_PROJ, TQ
    nt = seq // tm
    n = batch * seq
    row_spec = lambda w: pl.BlockSpec((tm, w), lambda b, t: (b * nt + t, 0))
    full = lambda a: pl.BlockSpec(a.shape, lambda b, t: (0,) * a.ndim)
    t_spec = pl.BlockSpec((1, SB_W, tm), lambda b, t: (b, 0, t))
    out_shape = (
        jax.ShapeDtypeStruct((n, SB_W), BF16),
        jax.ShapeDtypeStruct((batch, SB_W, seq), F32),
        jax.ShapeDtypeStruct((batch, SB_W, seq), F32),
        jax.ShapeDtypeStruct((n, SB_W), BF16),
        jax.ShapeDtypeStruct((batch, seq // tk, SB_W, tk), BF16),
        jax.ShapeDtypeStruct((n, LRU_W), BF16),
        jax.ShapeDtypeStruct((batch, 1, LRU_W), F32),
        jax.ShapeDtypeStruct((batch, CONV_W - 1, LRU_W), F32),
    )
    out_specs = (
        row_spec(SB_W), t_spec, t_spec, row_spec(SB_W),
        pl.BlockSpec((1, tm // tk, SB_W, tk), lambda b, t: (b, t, 0, 0)),
        row_spec(LRU_W),
        pl.BlockSpec((1, 1, LRU_W), lambda b, t: (b, 0, 0)),
        pl.BlockSpec((1, CONV_W - 1, LRU_W), lambda b, t: (b, 0, 0)),
    )
    args = (x2, g_mix, w_in_bf, conv_w, conv_b, w_gates, b_gates, lam, g_lru)
    return pl.pallas_call(
        functools.partial(_inproj_lru_kernel, tm=tm, tk=tk),
        grid=(batch, nt),
        in_specs=[row_spec(D_MODEL)] + [full(a) for a in args[1:]],
        out_specs=out_specs,
        out_shape=out_shape,
        scratch_shapes=[
            pltpu.VMEM((tm + SUBLANES, LRU_W), F32),
            pltpu.VMEM((tm, LRU_W), F32),
            pltpu.VMEM((tm, LRU_W), F32),
            pltpu.VMEM((SUBLANES, LRU_W), F32),
        ],
        compiler_params=pltpu.CompilerParams(
            dimension_semantics=("arbitrary", "arbitrary"), vmem_limit_bytes=VMEM_LIMIT),
        name="inproj_lru_prompt",
    )(*args)


def _inproj_lru_sample_kernel(x_ref, h0_ref, buf_ref, gmix_ref, win_ref, cw_ref, cb_ref, wg_ref, bg_ref,
                              lam_ref, glru_ref, q_ref, k_ref, v_ref, ylru_ref, ht_ref, cbuf_ref,
                              *, nseq, nt):
    hn = _rms(x_ref[...], gmix_ref[...])

    def proj(c):
        return _dot3(hn, win_ref[:, c * LRU_W:(c + 1) * LRU_W])

    xl = proj(0)
    gl = proj(1)
    q_ref[...] = proj(2)
    k_ref[...] = proj(3)
    v_ref[...] = proj(4)

    hist = [buf_ref[m * nseq:(m + 1) * nseq, :] for m in range(CONV_W - 1)]
    hist += [xl[t * nseq:(t + 1) * nseq, :] for t in range(nt)]
    cw = cw_ref[...]
    us = []
    for t in range(nt):
        ut = cb_ref[...] + cw[0:1, :] * hist[t]
        for j in range(1, CONV_W):
            ut = ut + cw[j:j + 1, :] * hist[t + j]
        us.append(ut)
    u = jnp.concatenate(us, axis=0)
    gates = _dot3(u, wg_ref[...]) + bg_ref[...]
    a, xin = _lru_gates(u, gates, lam_ref[...])
    h = h0_ref[...]
    hs = []
    for t in range(nt):
        h = a[t * nseq:(t + 1) * nseq, :] * h + xin[t * nseq:(t + 1) * nseq, :]
        hs.append(h)
    y = jnp.concatenate(hs, axis=0) * _gelu_tanh(gl)
    ylru_ref[...] = _rms(y, glru_ref[...])
    ht_ref[...] = h
    cbuf_ref[...] = jnp.concatenate(hist[nt:nt + CONV_W - 1], axis=0)


def _inproj_lru_sample(x_tm, h0, buf_tm, g_mix, w_in, conv_w, conv_b, w_gates, b_gates, lam, g_lru,
                       nseq, nt):
    n = nseq * nt
    out_shape = (
        jax.ShapeDtypeStruct((n, SB_W), F32),
        jax.ShapeDtypeStruct((n, SB_W), F32),
        jax.ShapeDtypeStruct((n, SB_W), F32),
        jax.ShapeDtypeStruct((n, LRU_W), F32),
        jax.ShapeDtypeStruct((nseq, LRU_W), F32),
        jax.ShapeDtypeStruct(((CONV_W - 1) * nseq, LRU_W), F32),
    )
    return pl.pallas_call(
        functools.partial(_inproj_lru_sample_kernel, nseq=nseq, nt=nt),
        out_shape=out_shape,
        compiler_params=pltpu.CompilerParams(vmem_limit_bytes=VMEM_LIMIT),
        name="inproj_lru_sample",
    )(x_tm, h0, buf_tm, g_mix, w_in, conv_w, conv_b, w_gates, b_gates, lam, g_lru)


def _sb_prompt_kernel(bsb_ref, q_ref, k_ref, vt_ref, g_ref, o_ref, acc_ref, c_ref, qm_ref, *, tq):
    i = pl.program_id(1)
    tk = tq
    rowk = lax.broadcasted_iota(I32, (tk, tq), 0)
    colq = lax.broadcasted_iota(I32, (tk, tq), 1)
    diag_mask = rowk < colq
    later_mat = (lax.broadcasted_iota(I32, (tk, tk), 1) > lax.broadcasted_iota(I32, (tk, tk), 0)).astype(BF16)
    lane = lax.broadcasted_iota(I32, (tq, LANES), 1)
    scale = jnp.asarray(HEAD_DIM ** -0.5, BF16)
    heads_per_tile = LANES // HEAD_DIM

    bf_zero = jnp.zeros((), BF16)
    for h in range(HEADS):
        pair, half = divmod(h, heads_per_tile)
        in_head = (lane >= half * HEAD_DIM) & (lane < (half + 1) * HEAD_DIM)
        qm = jnp.where(in_head, q_ref[:, pair * LANES:(pair + 1) * LANES] * scale, bf_zero)
        b_full = jnp.full((tq, LANES), bsb_ref[h], F32)
        b_hi = b_full.astype(BF16).astype(F32)
        b_ext = jnp.where(lane == 0, b_hi, jnp.where(lane == 1, b_full - b_hi, 0.0)).astype(BF16)
        qm_ref[h] = jnp.concatenate([qm, b_ext], axis=1)
    key_ones = jnp.where(lax.broadcasted_iota(I32, (tk, LANES), 1) < 2, 1.0, 0.0).astype(BF16)
    acc_ref[...] = jnp.zeros_like(acc_ref)
    c_ref[...] = jnp.zeros_like(c_ref)

    def blocks(js, masked):
        n_pairs = HEADS // heads_per_tile
        kbs = [[jnp.concatenate([k_ref[pl.ds(pl.multiple_of(j * tk, tk), tk), p * LANES:(p + 1) * LANES],
                                 key_ones], axis=1) for p in range(n_pairs)] for j in js]
        units = [(b, h) for b in range(len(js)) for h in range(HEADS)]
        ss = [lax.dot_general(kbs[b][h // heads_per_tile], qm_ref[h], (((1,), (1,)), ((), ())),
                              preferred_element_type=F32) for b, h in units]
        l1s, lbs = [], []
        for u in range(len(units)):
            l1, lb = _log_beta_terms(ss[u], accurate=False)
            if masked:
                l1 = jnp.where(diag_mask, l1, 0.0)
            l1s.append(l1.astype(BF16))
            lbs.append(lb)
        exs = [jnp.dot(later_mat, l1s[u], preferred_element_type=F32) for u in range(len(units))]
        totals = [exs[u][0:1, :] + l1s[u][0:1, :] for u in range(len(units))]
        ws = []
        for u, (b, h) in enumerate(units):
            c = c_ref[h:h + 1, :]
            if b == 1:
                c = c + totals[h]
            w = jnp.exp(lbs[u] + exs[u] + c)
            if masked:
                w = jnp.where(diag_mask, w, 0.0)
            ws.append(w.astype(BF16))
        for h in range(HEADS):
            rows = slice(h * HEAD_DIM, (h + 1) * HEAD_DIM)
            vts = jnp.concatenate([vt_ref[0, j, rows, :] for j in js], axis=1)
            wcat = jnp.concatenate([ws[b * HEADS + h] for b in range(len(js))], axis=0)
            acc_ref[rows, :] += jnp.dot(vts, wcat, preferred_element_type=F32)
            tot = totals[h] if len(js) == 1 else totals[h] + totals[HEADS + h]
            c_ref[h:h + 1, :] += tot

    blocks([i], True)

    @pl.when(lax.rem(i, 2) == 1)
    def _odd_block():
        blocks([i - 1], False)

    def body(p, carry):
        j = i - lax.rem(i, 2) - 1 - 2 * p
        blocks([j, j - 1], False)
        return carry

    lax.fori_loop(0, i // 2, body, 0)
    o_ref[...] = _rms(acc_ref[...].T, g_ref[...]).astype(BF16)


def _sb_prompt(b_sb, q_bf, k_bf, vt_bf, g_sb, batch, seq):
    tq = TQ
    nq = seq // tq
    n = batch * seq
    return pl.pallas_call(
        functools.partial(_sb_prompt_kernel, tq=tq),
        grid=(batch, nq),
        in_specs=[
            pl.BlockSpec(memory_space=pltpu.SMEM),
            pl.BlockSpec((tq, SB_W), lambda b, i: (b * nq + i, 0)),
            pl.BlockSpec((seq, SB_W), lambda b, i: (b, 0)),
            pl.BlockSpec((1, nq, SB_W, tq), lambda b, i: (b, 0, 0, 0)),
            pl.BlockSpec((1, SB_W), lambda b, i: (0, 0)),
        ],
        out_specs=pl.BlockSpec((tq, SB_W), lambda b, i: (b * nq + i, 0)),
        out_shape=jax.ShapeDtypeStruct((n, SB_W), BF16),
        scratch_shapes=[pltpu.VMEM((SB_W, tq), F32), pltpu.VMEM((HEADS, tq), F32),
                        pltpu.VMEM((HEADS, tq, 2 * LANES), BF16)],
        compiler_params=pltpu.CompilerParams(
            dimension_semantics=("arbitrary", "arbitrary"), vmem_limit_bytes=VMEM_LIMIT),
        name="sb_prompt",
    )(b_sb, q_bf, k_bf, vt_bf, g_sb)


def _sb_sample_kernel(pt_ref, bsb_ref, q_ref, kown_ref, vown_ref, g_ref, *rest, npg, nt, page):
    kp = rest[:npg]
    vp = rest[npg:2 * npg]
    o_ref = rest[2 * npg]
    qbd_ref, acc_ref, c_ref = rest[2 * npg + 1:]
    ci = pl.program_id(1)
    nc = pl.num_programs(1)
    rows = nt * HEADS
    rowi = lax.broadcasted_iota(I32, (rows, 1), 0)
    row_h = rowi & (HEADS - 1)
    row_t = rowi >> int(math.log2(HEADS))
    bias = jnp.zeros((rows, 1), F32)
    for h in range(HEADS):
        bias = jnp.where(row_h == h, bsb_ref[h], bias)
    head_of_lane = lax.broadcasted_iota(I32, (HEADS, SB_W), 1) // HEAD_DIM
    head_mask = head_of_lane == lax.broadcasted_iota(I32, (HEADS, SB_W), 0)

    @pl.when(ci == 0)
    def _init():
        q = q_ref[0] * (HEAD_DIM ** -0.5)
        qbd = jnp.concatenate(
            [jnp.where(head_mask, jnp.broadcast_to(q[t:t + 1, :], (HEADS, SB_W)), 0.0) for t in range(nt)],
            axis=0)
        qbd_ref[...] = jnp.concatenate(_split_bf16(qbd), axis=0)
        kown = kown_ref[0]
        vown = vown_ref[0]
        c = jnp.zeros((rows, 1), F32)
        acc = jnp.zeros((rows, SB_W), F32)
        for s in range(nt - 1, -1, -1):
            z = jnp.sum(qbd * kown[s:s + 1, :], axis=-1, keepdims=True) + bias
            vis = s < row_t
            l1, lb = _log_beta_terms(z, accurate=True)
            l1 = jnp.where(vis, l1, 0.0)
            w = jnp.where(vis, jnp.exp(lb + c), 0.0)
            acc = acc + w * vown[s:s + 1, :]
            c = c + l1
        acc_ref[...] = acc
        c_ref[...] = jnp.broadcast_to(c, (rows, LANES))

    later_mat = (lax.broadcasted_iota(I32, (page, page), 0) > lax.broadcasted_iota(I32, (page, page), 1)).astype(BF16)
    nt_dims = (((1,), (1,)), ((), ()))
    q2 = qbd_ref[...]
    q_hi = q2[:rows]

    def fold(x):
        return x[:rows] + x[rows:]

    ss = []
    for i in range(npg):
        kt_hi, kt_lo = _split_bf16(kp[i][0])
        ss.append(fold(jnp.dot(q2, kt_hi, preferred_element_type=F32))
                  + jnp.dot(q_hi, kt_lo, preferred_element_type=F32) + bias)
    l1s, lbs, exs = [], [], []
    for i in range(npg):
        l1, lb = _log_beta_terms(ss[i], accurate=True)
        l1s.append(l1)
        lbs.append(lb)
        exs.append(fold(jnp.dot(jnp.concatenate(_split_bf16(l1), axis=0), later_mat,
                                preferred_element_type=F32)))
    c = c_ref[...]
    acc = acc_ref[...]
    for i in range(npg):
        w = jnp.exp(lbs[i] + exs[i] + c)
        w_hi, w_lo = _split_bf16(w)
        vt_hi, vt_lo = _split_bf16(vp[i][0])
        acc = (acc + fold(lax.dot_general(jnp.concatenate([w_hi, w_lo], axis=0), vt_hi, nt_dims,
                                          preferred_element_type=F32))
               + lax.dot_general(w_hi, vt_lo, nt_dims, preferred_element_type=F32))
        c = c + jnp.broadcast_to(exs[i][:, 0:1] + l1s[i][:, 0:1], (rows, LANES))
    acc_ref[...] = acc
    c_ref[...] = c

    @pl.when(ci == nc - 1)
    def _fin():
        ys = [jnp.sum(jnp.where(head_mask, acc[t * HEADS:(t + 1) * HEADS, :], 0.0), axis=0, keepdims=True)
              for t in range(nt)]
        o_ref[0] = _rms(jnp.concatenate(ys, axis=0), g_ref[...])


def _sb_sample(page_table, b_sb, q_s, k_own, v_own, g_sb, cache_kt, cache_vt):
    nseq, npages = page_table.shape
    nt = q_s.shape[1]
    page = cache_kt.shape[2]
    npg = PAGES_PER_STEP
    nc = npages // npg
    rows = nt * HEADS

    def page_spec(i):
        return pl.BlockSpec(
            (1, SB_W, page),
            lambda s, c, pt, i=i: (pt[s * npages + (npages - 1) - (c * npg + i)], 0, 0))

    seq_spec = pl.BlockSpec((1, nt, SB_W), lambda s, c, pt: (s, 0, 0))
    grid_spec = pltpu.PrefetchScalarGridSpec(
        num_scalar_prefetch=1,
        grid=(nseq, nc),
        in_specs=[pl.BlockSpec(memory_space=pltpu.SMEM), seq_spec, seq_spec, seq_spec,
                  pl.BlockSpec((1, SB_W), lambda s, c, pt: (0, 0))]
                 + [page_spec(i) for i in range(npg)] + [page_spec(i) for i in range(npg)],
        out_specs=pl.BlockSpec((1, nt, SB_W), lambda s, c, pt: (s, 0, 0)),
        scratch_shapes=[pltpu.VMEM((2 * rows, SB_W), BF16), pltpu.VMEM((rows, SB_W), F32),
                        pltpu.VMEM((rows, LANES), F32)],
    )
    return pl.pallas_call(
        functools.partial(_sb_sample_kernel, npg=npg, nt=nt, page=page),
        grid_spec=grid_spec,
        out_shape=jax.ShapeDtypeStruct((nseq, nt, SB_W), F32),
        compiler_params=pltpu.CompilerParams(
            dimension_semantics=("arbitrary", "arbitrary"), vmem_limit_bytes=VMEM_LIMIT),
        name="sb_sample",
    )(page_table.reshape(-1), b_sb, q_s, k_own, v_own, g_sb, *([cache_kt] * npg), *([cache_vt] * npg))


def _outproj_router_kernel(x_ref, yl_ref, ys_ref, wout_ref, gffn_ref, wr_ref, br_ref,
                           x1_ref, h8_ref, e_ref, w_ref, *, tm, precise):
    lru_rows = slice(0, LRU_W)
    sb_rows = slice(LRU_W, LRU_W + SB_W)
    if precise:
        x1 = (x_ref[...] + _dot3(yl_ref[...], wout_ref[lru_rows, :]) + _dot3(ys_ref[...], wout_ref[sb_rows, :]))
    else:
        x1 = (x_ref[...]
              + jnp.dot(yl_ref[...], wout_ref[lru_rows, :], preferred_element_type=F32)
              + jnp.dot(ys_ref[...], wout_ref[sb_rows, :], preferred_element_type=F32))
    x1_ref[...] = x1
    h = _rms(x1, gffn_ref[...])
    for j in range(ROW_CHUNKS):
        h8_ref[pl.ds(j, tm, stride=ROW_CHUNKS), :] = h[:, j * LANES:(j + 1) * LANES]

    logits = _dot3(h, wr_ref[...]) + br_ref[...]

    lane = lax.broadcasted_iota(I32, (tm, ROUTER_COLS), 1)
    big = jnp.asarray(ROUTER_COLS, I32)
    neg = jnp.asarray(-jnp.inf, F32)
    gl = jnp.where(lane < N_GROUPS, logits, neg)
    gmax = jnp.max(gl, axis=-1, keepdims=True)
    g_idx = jnp.min(jnp.where(gl == gmax, lane, big), axis=-1, keepdims=True)
    p_g = 1.0 / jnp.sum(jnp.exp(gl - gmax), axis=-1, keepdims=True)
    e_lo = N_GROUPS + g_idx * EXPERTS_PER_GROUP
    el = jnp.where((lane >= e_lo) & (lane < e_lo + EXPERTS_PER_GROUP), logits, neg)
    m1 = jnp.max(el, axis=-1, keepdims=True)
    i1 = jnp.min(jnp.where(el == m1, lane, big), axis=-1, keepdims=True)
    el2 = jnp.where(lane == i1, neg, el)
    m2 = jnp.max(el2, axis=-1, keepdims=True)
    i2 = jnp.min(jnp.where(el2 == m2, lane, big), axis=-1, keepdims=True)
    r = jnp.exp(m2 - m1)
    w1 = p_g / (1.0 + r)
    w2 = w1 * r
    e_ref[...] = jnp.where(lane == 0, i1 - N_GROUPS, jnp.where(lane == 1, i2 - N_GROUPS, 0))
    w_ref[...] = jnp.where(lane == 0, w1, jnp.where(lane == 1, w2, 0.0))


def _outproj_router(x2, ylru, ysb, w_out, g_ffn, w_r, b_r, tm):
    n = x2.shape[0]
    precise = ylru.dtype == F32
    assert ysb.dtype == ylru.dtype == w_out.dtype
    row_spec = lambda w: pl.BlockSpec((tm, w), lambda t: (t, 0))
    full = lambda a: pl.BlockSpec(a.shape, lambda t: (0,) * a.ndim)
    return pl.pallas_call(
        functools.partial(_outproj_router_kernel, tm=tm, precise=precise),
        grid=(n // tm,),
        in_specs=[row_spec(D_MODEL), row_spec(LRU_W), row_spec(SB_W), full(w_out), full(g_ffn), full(w_r),
                  full(b_r)],
        out_specs=(row_spec(D_MODEL), pl.BlockSpec((tm * ROW_CHUNKS, LANES), lambda t: (t, 0)),
                   row_spec(ROUTER_COLS), row_spec(ROUTER_COLS)),
        out_shape=(jax.ShapeDtypeStruct((n, D_MODEL), F32),
                   jax.ShapeDtypeStruct((n * ROW_CHUNKS, LANES), F32),
                   jax.ShapeDtypeStruct((n, ROUTER_COLS), I32),
                   jax.ShapeDtypeStruct((n, ROUTER_COLS), F32)),
        compiler_params=pltpu.CompilerParams(
            dimension_semantics=("arbitrary",), vmem_limit_bytes=VMEM_LIMIT),
        name="outproj_router",
    )(x2, ylru, ysb, w_out, g_ffn, w_r, b_r)


def _row_tile(ref, row):
    return ref.at[pl.ds(pl.multiple_of(row * ROW_CHUNKS, ROW_CHUNKS), ROW_CHUNKS)]


def _for_each_row(td, fn):
    def body(r8, c):
        for k in range(SUBLANES):
            fn(r8 * SUBLANES + k)
        return c
    lax.fori_loop(0, td // SUBLANES, body, 0)


def _dispatch_kernel(pos_ref, h8_ref, xs_in_hbm, xs_hbm, stage_ref, sem, *, td):
    del xs_in_hbm
    t = pl.program_id(0)
    nt = pl.num_programs(0)
    slot = lax.rem(t, 2)

    def wait_slot(s):
        for _ in range(2):
            pltpu.make_async_copy(stage_ref.at[s], xs_hbm.at[pl.ds(0, td * ROW_CHUNKS)], sem.at[s]).wait()

    @pl.when(t >= 2)
    def _slot_free():
        wait_slot(slot)

    stage_ref[slot] = h8_ref[...]

    def start(r):
        for k in range(2):
            pltpu.make_async_copy(_row_tile(stage_ref.at[slot], r), _row_tile(xs_hbm, pos_ref[0, 0, 2 * r + k]),
                                  sem.at[slot]).start()

    _for_each_row(td, start)

    @pl.when(t == nt - 1)
    def _drain():
        @pl.when(t >= 1)
        def _():
            wait_slot(1 - slot)
        wait_slot(slot)


def _dispatch(pos, h8, xs8, td):
    n = pos.shape[0]
    pos3 = pos.reshape(n // td, 1, 2 * td)
    return pl.pallas_call(
        functools.partial(_dispatch_kernel, td=td),
        grid=(n // td,),
        in_specs=[pl.BlockSpec((1, 1, 2 * td), lambda t: (t, 0, 0), memory_space=pltpu.SMEM),
                  pl.BlockSpec((td * ROW_CHUNKS, LANES), lambda t: (t, 0)),
                  pl.BlockSpec(memory_space=pl.ANY)],
        out_specs=pl.BlockSpec(memory_space=pl.ANY),
        out_shape=jax.ShapeDtypeStruct(xs8.shape, xs8.dtype),
        scratch_shapes=[pltpu.VMEM((2, td * ROW_CHUNKS, LANES), F32), pltpu.SemaphoreType.DMA((2,))],
        input_output_aliases={2: 0},
        compiler_params=pltpu.CompilerParams(
            dimension_semantics=("arbitrary",), vmem_limit_bytes=VMEM_LIMIT),
        name="dispatch",
    )(pos3, h8, xs8)


def _experts_kernel(te_ref, tv_ref, tf_ref, xs_ref, wg_ref, wu_ref, wd_ref, ys_ref,
                    x2_ref, wgb_ref, wub_ref, wdb_ref, *, tm):
    del te_ref
    t = pl.program_id(0)

    @pl.when(tf_ref[t] == 1)
    def _cast_weights():
        wgb_ref[...] = wg_ref[0].astype(BF16)
        wub_ref[...] = wu_ref[0].astype(BF16)
        wdb_ref[...] = wd_ref[0].astype(BF16)

    @pl.when(tv_ref[t] == 1)
    def _compute():
        for j in range(ROW_CHUNKS):
            x2_ref[:, j * LANES:(j + 1) * LANES] = xs_ref[pl.ds(j, tm, stride=ROW_CHUNKS), :].astype(BF16)
        x = x2_ref[...]
        g = jnp.dot(x, wgb_ref[...], preferred_element_type=F32)
        u = jnp.dot(x, wub_ref[...], preferred_element_type=F32)
        act = (g * jax.nn.sigmoid(g) * u).astype(BF16)
        y = jnp.dot(act, wdb_ref[...], preferred_element_type=F32)
        for j in range(ROW_CHUNKS):
            ys_ref[pl.ds(j, tm, stride=ROW_CHUNKS), :] = y[:, j * LANES:(j + 1) * LANES]

    @pl.when(tv_ref[t] == 0)
    def _pad():
        ys_ref[...] = jnp.zeros_like(ys_ref)


def _experts(tile_e, tile_valid, tile_first, xs8, w_eg, w_eu, w_ed, tm):
    n_tiles = tile_e.shape[0]
    grid_spec = pltpu.PrefetchScalarGridSpec(
        num_scalar_prefetch=3,
        grid=(n_tiles,),
        in_specs=[
            pl.BlockSpec((tm * ROW_CHUNKS, LANES), lambda t, te, tv, tf: (t, 0)),
            pl.BlockSpec((1, D_MODEL, D_EXPERT), lambda t, te, tv, tf: (te[t], 0, 0)),
            pl.BlockSpec((1, D_MODEL, D_EXPERT), lambda t, te, tv, tf: (te[t], 0, 0)),
            pl.BlockSpec((1, D_EXPERT, D_MODEL), lambda t, te, tv, tf: (te[t], 0, 0)),
        ],
        out_specs=pl.BlockSpec((tm * ROW_CHUNKS, LANES), lambda t, te, tv, tf: (t, 0)),
        scratch_shapes=[pltpu.VMEM((tm, D_MODEL), BF16), pltpu.VMEM((D_MODEL, D_EXPERT), BF16),
                        pltpu.VMEM((D_MODEL, D_EXPERT), BF16), pltpu.VMEM((D_EXPERT, D_MODEL), BF16)],
    )
    return pl.pallas_call(
        functools.partial(_experts_kernel, tm=tm),
        grid_spec=grid_spec,
        out_shape=jax.ShapeDtypeStruct(xs8.shape, F32),
        compiler_params=pltpu.CompilerParams(
            dimension_semantics=("arbitrary",), vmem_limit_bytes=VMEM_LIMIT),
        name="experts",
    )(tile_e, tile_valid, tile_first, xs8, w_eg, w_eu, w_ed)


def _combine_kernel(pos_ref, posn_ref, x1_ref, wsel_ref, gfin_ref, ys_hbm, o_ref, buf_ref, sem, *, td):
    t = pl.program_id(0)
    nt = pl.num_programs(0)
    slot = lax.rem(t, 2)

    def start_gather(idx_ref, s):
        def start(r):
            for k in range(2):
                pltpu.make_async_copy(_row_tile(ys_hbm, idx_ref[0, 0, 2 * r + k]), _row_tile(buf_ref.at[s, k], r),
                                      sem.at[s]).start()
        _for_each_row(td, start)

    @pl.when(t == 0)
    def _first():
        start_gather(pos_ref, slot)

    @pl.when(t + 1 < nt)
    def _next():
        start_gather(posn_ref, 1 - slot)

    for k in range(2):
        pltpu.make_async_copy(ys_hbm.at[pl.ds(0, td * ROW_CHUNKS)], buf_ref.at[slot, k], sem.at[slot]).wait()

    w = wsel_ref[...]
    w0 = w[:, 0:1]
    w1 = w[:, 1:2]
    cols = []
    for j in range(ROW_CHUNKS):
        cols.append(x1_ref[:, j * LANES:(j + 1) * LANES]
                    + w0 * buf_ref[slot, 0, pl.ds(j, td, stride=ROW_CHUNKS), :]
                    + w1 * buf_ref[slot, 1, pl.ds(j, td, stride=ROW_CHUNKS), :])
    o_ref[...] = _rms(jnp.concatenate(cols, axis=1), gfin_ref[...])


def _combine_final(pos, x1, wsel, g_final, ys8, td):
    n = pos.shape[0]
    nt = n // td
    pos3 = pos.reshape(nt, 1, 2 * td)
    idx_spec = lambda fn: pl.BlockSpec((1, 1, 2 * td), fn, memory_space=pltpu.SMEM)
    return pl.pallas_call(
        functools.partial(_combine_kernel, td=td),
        grid=(nt,),
        in_specs=[idx_spec(lambda t: (t, 0, 0)), idx_spec(lambda t: (jnp.minimum(t + 1, nt - 1), 0, 0)),
                  pl.BlockSpec((td, D_MODEL), lambda t: (t, 0)),
                  pl.BlockSpec((td, ROUTER_COLS), lambda t: (t, 0)),
                  pl.BlockSpec((1, D_MODEL), lambda t: (0, 0)),
                  pl.BlockSpec(memory_space=pl.ANY)],
        out_specs=pl.BlockSpec((td, D_MODEL), lambda t: (t, 0)),
        out_shape=jax.ShapeDtypeStruct((n, D_MODEL), F32),
        scratch_shapes=[pltpu.VMEM((2, 2, td * ROW_CHUNKS, LANES), F32), pltpu.SemaphoreType.DMA((2,))],
        compiler_params=pltpu.CompilerParams(
            dimension_semantics=("arbitrary",), vmem_limit_bytes=VMEM_LIMIT),
        name="combine_final",
    )(pos3, pos3, x1, wsel, g_final, ys8)


def _routing_plan(e_sel, tm, n_tiles):
    e_flat = e_sel.reshape(-1)
    onehot = (e_flat[:, None] == jnp.arange(N_EXPERTS, dtype=I32)[None, :]).astype(I32)
    csum = jnp.cumsum(onehot, axis=0)
    rank = jnp.sum(onehot * csum, axis=1) - 1
    counts = csum[-1]
    tiles_per = (counts + tm - 1) // tm
    tile_end = jnp.cumsum(tiles_per)
    tile_start = tile_end - tiles_per
    pos = (tile_start * tm)[e_flat] + rank
    total = tile_end[-1]
    t = jnp.arange(n_tiles, dtype=I32)
    tile_e = jnp.minimum(jnp.sum((t[:, None] >= tile_end[None, :]).astype(I32), axis=1), N_EXPERTS - 1)
    last_e = jnp.max(jnp.where(counts > 0, jnp.arange(N_EXPERTS, dtype=I32), 0))
    valid = t < total
    tile_e = jnp.where(valid, tile_e, last_e).astype(I32)
    first = (valid & (t == tile_start[tile_e])).astype(I32)
    return pos.reshape(-1, 2).astype(I32), tile_e, valid.astype(I32), first


def _block_diag(w):
    nb, d, _ = w.shape
    eye = jnp.eye(nb, dtype=w.dtype)
    return (eye[:, None, :, None] * w[:, :, None, :]).reshape(nb * d, nb * d)


def kernel(x_prompt, x_sample, cache_k, cache_v, state_lru_h, state_conv, page_table, g_mix, w_in, conv_w, conv_b, w_a, b_a, w_i, b_i, lam, b_sb, g_out_lru, g_out_sb, w_out, g_ffn, w_rg, b_rg, w_re, b_re, w_eg, w_eu, w_ed, g_final):
    depth = w_in.shape[0]
    assert depth == 1, "single-layer trunk"
    batch, seq, _ = x_prompt.shape
    nseq, nt, _ = x_sample.shape
    npool, page = cache_k.shape[1], cache_k.shape[2]
    n_p = batch * seq
    n_s = nseq * nt
    row = lambda a: a.reshape(1, -1)

    w_gates = jnp.concatenate([_block_diag(w_a[0]), _block_diag(w_i[0])], axis=1)
    b_gates = jnp.concatenate([b_a[0], b_i[0]]).reshape(1, -1)
    w_r = jnp.zeros((D_MODEL, ROUTER_COLS), F32)
    w_r = w_r.at[:, :N_GROUPS].set(w_rg[0]).at[:, N_GROUPS:N_GROUPS + N_EXPERTS].set(w_re[0])
    b_r = jnp.zeros((1, ROUTER_COLS), F32)
    b_r = b_r.at[0, :N_GROUPS].set(b_rg[0]).at[0, N_GROUPS:N_GROUPS + N_EXPERTS].set(b_re[0])
    lru_tail = (row(lam[0]), row(g_out_lru[0]))

    x_p = x_prompt.reshape(n_p, D_MODEL)
    q_p, kt_p, vt_p, kb_p, vtb_p, ylru_p, ht_p, cbuf_p = _inproj_lru_prompt(
        x_p, row(g_mix[0]), w_in[0].astype(BF16), conv_w[0], row(conv_b[0]), w_gates.astype(BF16), b_gates,
        *lru_tail, batch, seq)
    ysb_p = _sb_prompt(b_sb[0], q_p, kb_p, vtb_p, row(g_out_sb[0]), batch, seq)

    x_s = x_sample.transpose(1, 0, 2).reshape(n_s, D_MODEL)
    buf_tm = state_conv[0].transpose(1, 0, 2).reshape((CONV_W - 1) * nseq, LRU_W)
    q_s, k_s, v_s, ylru_s, ht_s, cbuf_s = _inproj_lru_sample(
        x_s, state_lru_h[0], buf_tm, row(g_mix[0]), w_in[0], conv_w[0], row(conv_b[0]), w_gates, b_gates,
        *lru_tail, nseq, nt)
    seq_major = lambda a, w: a.reshape(-1, nseq, w).transpose(1, 0, 2)
    q_sm, k_sm, v_sm = seq_major(q_s, SB_W), seq_major(k_s, SB_W), seq_major(v_s, SB_W)
    pages_t = lambda c: c[0].transpose(0, 2, 3, 1).reshape(npool, SB_W, page)
    ysb_s = _sb_sample(page_table, b_sb[0], q_sm, k_sm, v_sm, row(g_out_sb[0]),
                       pages_t(cache_k), pages_t(cache_v))
    x_s2 = x_sample.reshape(n_s, D_MODEL)
    ylru_s2 = seq_major(ylru_s, LRU_W).reshape(n_s, LRU_W)
    ysb_s2 = ysb_s.reshape(n_s, SB_W)

    router_args = (row(g_ffn[0]), w_r, b_r)
    x1_p, h8_p, e_p, wsel_p = _outproj_router(x_p, ylru_p, ysb_p, w_out[0].astype(BF16), *router_args, TM_OUT)
    x1_s, h8_s, e_s, wsel_s = _outproj_router(x_s2, ylru_s2, ysb_s2, w_out[0], *router_args, n_s)

    n_all = n_p + n_s
    n_tiles = -(-2 * n_all // TM_EXP) + N_EXPERTS
    e_all = jnp.concatenate([e_p[:, :2], e_s[:, :2]], axis=0)
    pos, tile_e, tile_valid, tile_first = _routing_plan(e_all, TM_EXP, n_tiles)
    pos_p, pos_s = pos[:n_p], pos[n_p:]

    xs8 = jnp.zeros((n_tiles * TM_EXP * ROW_CHUNKS, LANES), F32)
    xs8 = _dispatch(pos_p, h8_p, xs8, TD)
    xs8 = _dispatch(pos_s, h8_s, xs8, n_s)
    ys8 = _experts(tile_e, tile_valid, tile_first, xs8, w_eg[0], w_eu[0], w_ed[0], TM_EXP)

    gfin = row(g_final)
    y_p = _combine_final(pos_p, x1_p, wsel_p, gfin, ys8, TD)
    y_s = _combine_final(pos_s, x1_s, wsel_s, gfin, ys8, n_s)

    heads = lambda a, lead: a.reshape(*lead, HEADS, HEAD_DIM)
    from_t = lambda a: a.reshape(batch, HEADS, HEAD_DIM, seq).transpose(0, 3, 1, 2)[None]
    return (
        y_p.reshape(batch, seq, D_MODEL),
        y_s.reshape(nseq, nt, D_MODEL),
        from_t(kt_p),
        from_t(vt_p),
        ht_p.reshape(1, batch, LRU_W),
        cbuf_p.reshape(1, batch, CONV_W - 1, LRU_W),
        heads(k_sm, (1, nseq, nt)),
        heads(v_sm, (1, nseq, nt)),
        ht_s.reshape(1, nseq, LRU_W),
        cbuf_s.reshape(CONV_W - 1, nseq, LRU_W).transpose(1, 0, 2)[None],
    )
```

```python
import functools
import math

import jax
import jax.numpy as jnp
from jax import lax
from jax.experimental import pallas as pl
from jax.experimental.pallas import tpu as pltpu

F32 = jnp.float32
BF16 = jnp.bfloat16
I32 = jnp.int32

D_MODEL = 1024
LRU_W = 512
SB_W = 512
HEADS = 8
HEAD_DIM = 64
CONV_W = 4
LRU_C = 8.0
EPS = 1e-6
N_GROUPS = 4
EXPERTS_PER_GROUP = 8
N_EXPERTS = 32
D_EXPERT = 512

LANES = 128
SUBLANES = 8
ROW_CHUNKS = D_MODEL // LANES

TM_PROJ = 512
TQ = 256
TM_OUT = 512
TM_EXP = 256
TD = 512
PAGES_PER_STEP = 32
ROUTER_COLS = LANES
VMEM_LIMIT = 56 * 1024 * 1024


def _softplus(z):
    return jnp.maximum(z, 0.0) + jnp.log1p(jnp.exp(-jnp.abs(z)))


def _log_beta_terms(z, accurate):
    t = jnp.exp(-jnp.abs(z))
    log_beta = jnp.minimum(z, 0.0) - (jnp.log1p(t) if accurate else jnp.log(1.0 + t))
    return log_beta - z, log_beta


def _rms(x, g):
    return x * lax.rsqrt(jnp.mean(x * x, axis=-1, keepdims=True) + EPS) * g


def _gelu_tanh(x):
    c = math.sqrt(2.0 / math.pi)
    return x * (0.5 * (1.0 + jnp.tanh(c * (x + 0.044715 * (x * x * x)))))


def _neg_expm1(x):
    u = jnp.exp(x)
    near = x > -1.0
    ratio = x / jnp.where(near & (u < 1.0), jnp.log(u), -1.0)
    return jnp.where(near, jnp.where(u < 1.0, (1.0 - u) * ratio, -x), 1.0 - u)


def _lru_gates(u, gates, lam):
    r = jax.nn.sigmoid(gates[:, :LRU_W])
    i = jax.nn.sigmoid(gates[:, LRU_W:])
    log_a = (-LRU_C) * r * _softplus(-lam)
    a = jnp.exp(log_a)
    xin = jnp.sqrt(_neg_expm1(2.0 * log_a)) * (i * u)
    return a, xin


def _split_bf16(x):
    hi = x.astype(BF16)
    lo = (x - hi.astype(F32)).astype(BF16)
    return hi, lo


def _dot3(x, w):
    x_hi, x_lo = _split_bf16(x)
    w_hi, w_lo = _split_bf16(w)
    n = x.shape[0]
    top = jnp.dot(jnp.concatenate([x_hi, x_lo], axis=0), w_hi, preferred_element_type=F32)
    return top[:n] + top[n:] + jnp.dot(x_hi, w_lo, preferred_element_type=F32)


def _inproj_lru_kernel(x_ref, gmix_ref, win_ref, cw_ref, cb_ref, wg_ref, bg_ref, lam_ref, glru_ref,
                       q_ref, kt_ref, vt_ref, kb_ref, vtb_ref, ylru_ref, ht_ref, cbuf_ref,
                       xlp_ref, a_ref, s_ref, h_ref, *, tm, tk):
    t = pl.program_id(1)
    nt = pl.num_programs(1)

    @pl.when(t == 0)
    def _init():
        xlp_ref[0:SUBLANES, :] = jnp.zeros((SUBLANES, LRU_W), F32)
        h_ref[...] = jnp.zeros_like(h_ref)

    hn = _rms(x_ref[...], gmix_ref[...]).astype(BF16)

    def proj(c):
        return jnp.dot(hn, win_ref[:, c * LRU_W:(c + 1) * LRU_W], preferred_element_type=F32)

    xl = proj(0)
    gl = proj(1)
    q_ref[...] = proj(2).astype(BF16)
    k = proj(3)
    kt_ref[0] = k.T
    kb_ref[...] = k.astype(BF16)
    v_t = proj(4).T
    vt_ref[0] = v_t
    for c in range(tm // tk):
        vtb_ref[0, c] = v_t[:, c * tk:(c + 1) * tk].astype(BF16)

    xlp_ref[SUBLANES:SUBLANES + tm, :] = xl
    cw = cw_ref[...]
    u = cb_ref[...] + cw[3:4, :] * xl
    for j in range(1, CONV_W):
        u = u + cw[3 - j:4 - j, :] * xlp_ref[pl.ds(SUBLANES - j, tm), :]
    xlp_ref[0:SUBLANES, :] = xl[tm - SUBLANES:tm, :]

    gates = jnp.dot(u.astype(BF16), wg_ref[...], preferred_element_type=F32) + bg_ref[...]
    a, xin = _lru_gates(u, gates, lam_ref[...])
    a_ref[...] = a
    s_ref[...] = xin

    row = lax.broadcasted_iota(I32, (SUBLANES, LRU_W), 0)

    def group(g, h):
        r0 = pl.multiple_of(g * SUBLANES, SUBLANES)
        av = a_ref[pl.ds(r0, SUBLANES), :]
        xv = s_ref[pl.ds(r0, SUBLANES), :]
        for d in (1, 2, 4):
            valid = row >= d
            xv = jnp.where(valid, xv + av * pltpu.roll(xv, d, 0), xv)
            av = jnp.where(valid, av * pltpu.roll(av, d, 0), av)
        hv = xv + av * h
        s_ref[pl.ds(r0, SUBLANES), :] = hv
        return hv[SUBLANES - 1:SUBLANES, :]

    h = lax.fori_loop(0, tm // SUBLANES, group, h_ref[0:1, :], unroll=4)
    h_ref[0:1, :] = h

    y = s_ref[...] * _gelu_tanh(gl)
    ylru_ref[...] = _rms(y, glru_ref[...]).astype(BF16)

    @pl.when(t == nt - 1)
    def _fin():
        ht_ref[0] = h
        cbuf_ref[0] = xl[tm - (CONV_W - 1):tm, :]


def _inproj_lru_prompt(x2, g_mix, w_in_bf, conv_w, conv_b, w_gates, b_gates, lam, g_lru, batch, seq):
    tm, tk = TM_PROJ, TQ
    nt = seq // tm
    n = batch * seq
    row_spec = lambda w: pl.BlockSpec((tm, w), lambda b, t: (b * nt + t, 0))
    full = lambda a: pl.BlockSpec(a.shape, lambda b, t: (0,) * a.ndim)
    t_spec = pl.BlockSpec((1, SB_W, tm), lambda b, t: (b, 0, t))
    out_shape = (
        jax.ShapeDtypeStruct((n, SB_W), BF16),
        jax.ShapeDtypeStruct((batch, SB_W, seq), F32),
        jax.ShapeDtypeStruct((batch, SB_W, seq), F32),
        jax.ShapeDtypeStruct((n, SB_W), BF16),
        jax.ShapeDtypeStruct((batch, seq // tk, SB_W, tk), BF16),
        jax.ShapeDtypeStruct((n, LRU_W), BF16),
        jax.ShapeDtypeStruct((batch, 1, LRU_W), F32),
        jax.ShapeDtypeStruct((batch, CONV_W - 1, LRU_W), F32),
    )
    out_specs = (
        row_spec(SB_W), t_spec, t_spec, row_spec(SB_W),
        pl.BlockSpec((1, tm // tk, SB_W, tk), lambda b, t: (b, t, 0, 0)),
        row_spec(LRU_W),
        pl.BlockSpec((1, 1, LRU_W), lambda b, t: (b, 0, 0)),
        pl.BlockSpec((1, CONV_W - 1, LRU_W), lambda b, t: (b, 0, 0)),
    )
    args = (x2, g_mix, w_in_bf, conv_w, conv_b, w_gates, b_gates, lam, g_lru)
    return pl.pallas_call(
        functools.partial(_inproj_lru_kernel, tm=tm, tk=tk),
        grid=(batch, nt),
        in_specs=[row_spec(D_MODEL)] + [full(a) for a in args[1:]],
        out_specs=out_specs,
        out_shape=out_shape,
        scratch_shapes=[
            pltpu.VMEM((tm + SUBLANES, LRU_W), F32),
            pltpu.VMEM((tm, LRU_W), F32),
            pltpu.VMEM((tm, LRU_W), F32),
            pltpu.VMEM((SUBLANES, LRU_W), F32),
        ],
        compiler_params=pltpu.CompilerParams(
            dimension_semantics=("arbitrary", "arbitrary"), vmem_limit_bytes=VMEM_LIMIT),
        name="inproj_lru_prompt",
    )(*args)


def _inproj_lru_sample_kernel(x_ref, h0_ref, buf_ref, gmix_ref, win_ref, cw_ref, cb_ref, wg_ref, bg_ref,
                              lam_ref, glru_ref, q_ref, k_ref, v_ref, ylru_ref, ht_ref, cbuf_ref,
                              *, nseq, nt):
    hn = _rms(x_ref[...], gmix_ref[...])

    def proj(c):
        return _dot3(hn, win_ref[:, c * LRU_W:(c + 1) * LRU_W])

    xl = proj(0)
    gl = proj(1)
    q_ref[...] = proj(2)
    k_ref[...] = proj(3)
    v_ref[...] = proj(4)

    hist = [buf_ref[m * nseq:(m + 1) * nseq, :] for m in range(CONV_W - 1)]
    hist += [xl[t * nseq:(t + 1) * nseq, :] for t in range(nt)]
    cw = cw_ref[...]
    us = []
    for t in range(nt):
        ut = cb_ref[...] + cw[0:1, :] * hist[t]
        for j in range(1, CONV_W):
            ut = ut + cw[j:j + 1, :] * hist[t + j]
        us.append(ut)
    u = jnp.concatenate(us, axis=0)
    gates = _dot3(u, wg_ref[...]) + bg_ref[...]
    a, xin = _lru_gates(u, gates, lam_ref[...])
    h = h0_ref[...]
    hs = []
    for t in range(nt):
        h = a[t * nseq:(t + 1) * nseq, :] * h + xin[t * nseq:(t + 1) * nseq, :]
        hs.append(h)
    y = jnp.concatenate(hs, axis=0) * _gelu_tanh(gl)
    ylru_ref[...] = _rms(y, glru_ref[...])
    ht_ref[...] = h
    cbuf_ref[...] = jnp.concatenate(hist[nt:nt + CONV_W - 1], axis=0)


def _inproj_lru_sample(x_tm, h0, buf_tm, g_mix, w_in, conv_w, conv_b, w_gates, b_gates, lam, g_lru,
                       nseq, nt):
    n = nseq * nt
    out_shape = (
        jax.ShapeDtypeStruct((n, SB_W), F32),
        jax.ShapeDtypeStruct((n, SB_W), F32),
        jax.ShapeDtypeStruct((n, SB_W), F32),
        jax.ShapeDtypeStruct((n, LRU_W), F32),
        jax.ShapeDtypeStruct((nseq, LRU_W), F32),
        jax.ShapeDtypeStruct(((CONV_W - 1) * nseq, LRU_W), F32),
    )
    return pl.pallas_call(
        functools.partial(_inproj_lru_sample_kernel, nseq=nseq, nt=nt),
        out_shape=out_shape,
        compiler_params=pltpu.CompilerParams(vmem_limit_bytes=VMEM_LIMIT),
        name="inproj_lru_sample",
    )(x_tm, h0, buf_tm, g_mix, w_in, conv_w, conv_b, w_gates, b_gates, lam, g_lru)


def _sb_prompt_kernel(bsb_ref, q_ref, k_ref, vt_ref, g_ref, o_ref, acc_ref, c_ref, qm_ref, *, tq):
    i = pl.program_id(1)
    tk = tq
    rowk = lax.broadcasted_iota(I32, (tk, tq), 0)
    colq = lax.broadcasted_iota(I32, (tk, tq), 1)
    diag_mask = rowk < colq
    later_mat = (lax.broadcasted_iota(I32, (tk, tk), 1) > lax.broadcasted_iota(I32, (tk, tk), 0)).astype(BF16)
    lane = lax.broadcasted_iota(I32, (tq, LANES), 1)
    scale = jnp.asarray(HEAD_DIM ** -0.5, BF16)
    heads_per_tile = LANES // HEAD_DIM

    bf_zero = jnp.zeros((), BF16)
    for h in range(HEADS):
        pair, half = divmod(h, heads_per_tile)
        in_head = (lane >= half * HEAD_DIM) & (lane < (half + 1) * HEAD_DIM)
        qm = jnp.where(in_head, q_ref[:, pair * LANES:(pair + 1) * LANES] * scale, bf_zero)
        b_full = jnp.full((tq, LANES), bsb_ref[h], F32)
        b_hi = b_full.astype(BF16).astype(F32)
        b_ext = jnp.where(lane == 0, b_hi, jnp.where(lane == 1, b_full - b_hi, 0.0)).astype(BF16)
        qm_ref[h] = jnp.concatenate([qm, b_ext], axis=1)
    key_ones = jnp.where(lax.broadcasted_iota(I32, (tk, LANES), 1) < 2, 1.0, 0.0).astype(BF16)
    acc_ref[...] = jnp.zeros_like(acc_ref)
    c_ref[...] = jnp.zeros_like(c_ref)

    def blocks(js, masked):
        n_pairs = HEADS // heads_per_tile
        kbs = [[jnp.concatenate([k_ref[pl.ds(pl.multiple_of(j * tk, tk), tk), p * LANES:(p + 1) * LANES],
                                 key_ones], axis=1) for p in range(n_pairs)] for j in js]
        units = [(b, h) for b in range(len(js)) for h in range(HEADS)]
        ss = [lax.dot_general(kbs[b][h // heads_per_tile], qm_ref[h], (((1,), (1,)), ((), ())),
                              preferred_element_type=F32) for b, h in units]
        l1s, lbs = [], []
        for u in range(len(units)):
            l1, lb = _log_beta_terms(ss[u], accurate=False)
            if masked:
                l1 = jnp.where(diag_mask, l1, 0.0)
            l1s.append(l1.astype(BF16))
            lbs.append(lb)
        exs = [jnp.dot(later_mat, l1s[u], preferred_element_type=F32) for u in range(len(units))]
        totals = [exs[u][0:1, :] + l1s[u][0:1, :] for u in range(len(units))]
        ws = []
        for u, (b, h) in enumerate(units):
            c = c_ref[h:h + 1, :]
            if b == 1:
                c = c + totals[h]
            w = jnp.exp(lbs[u] + exs[u] + c)
            if masked:
                w = jnp.where(diag_mask, w, 0.0)
            ws.append(w.astype(BF16))
        for h in range(HEADS):
            rows = slice(h * HEAD_DIM, (h + 1) * HEAD_DIM)
            vts = jnp.concatenate([vt_ref[0, j, rows, :] for j in js], axis=1)
            wcat = jnp.concatenate([ws[b * HEADS + h] for b in range(len(js))], axis=0)
            acc_ref[rows, :] += jnp.dot(vts, wcat, preferred_element_type=F32)
            tot = totals[h] if len(js) == 1 else totals[h] + totals[HEADS + h]
            c_ref[h:h + 1, :] += tot

    blocks([i], True)

    @pl.when(lax.rem(i, 2) == 1)
    def _odd_block():
        blocks([i - 1], False)

    def body(p, carry):
        j = i - lax.rem(i, 2) - 1 - 2 * p
        blocks([j, j - 1], False)
        return carry

    lax.fori_loop(0, i // 2, body, 0)
    o_ref[...] = _rms(acc_ref[...].T, g_ref[...]).astype(BF16)


def _sb_prompt(b_sb, q_bf, k_bf, vt_bf, g_sb, batch, seq):
    tq = TQ
    nq = seq // tq
    n = batch * seq
    return pl.pallas_call(
        functools.partial(_sb_prompt_kernel, tq=tq),
        grid=(batch, nq),
        in_specs=[
            pl.BlockSpec(memory_space=pltpu.SMEM),
            pl.BlockSpec((tq, SB_W), lambda b, i: (b * nq + i, 0)),
            pl.BlockSpec((seq, SB_W), lambda b, i: (b, 0)),
            pl.BlockSpec((1, nq, SB_W, tq), lambda b, i: (b, 0, 0, 0)),
            pl.BlockSpec((1, SB_W), lambda b, i: (0, 0)),
        ],
        out_specs=pl.BlockSpec((tq, SB_W), lambda b, i: (b * nq + i, 0)),
        out_shape=jax.ShapeDtypeStruct((n, SB_W), BF16),
        scratch_shapes=[pltpu.VMEM((SB_W, tq), F32), pltpu.VMEM((HEADS, tq), F32),
                        pltpu.VMEM((HEADS, tq, 2 * LANES), BF16)],
        compiler_params=pltpu.CompilerParams(
            dimension_semantics=("arbitrary", "arbitrary"), vmem_limit_bytes=VMEM_LIMIT),
        name="sb_prompt",
    )(b_sb, q_bf, k_bf, vt_bf, g_sb)


def _sb_sample_kernel(pt_ref, bsb_ref, q_ref, kown_ref, vown_ref, g_ref, *rest, npg, nt, page):
    kp = rest[:npg]
    vp = rest[npg:2 * npg]
    o_ref = rest[2 * npg]
    qbd_ref, acc_ref, c_ref = rest[2 * npg + 1:]
    ci = pl.program_id(1)
    nc = pl.num_programs(1)
    rows = nt * HEADS
    rowi = lax.broadcasted_iota(I32, (rows, 1), 0)
    row_h = rowi & (HEADS - 1)
    row_t = rowi >> int(math.log2(HEADS))
    bias = jnp.zeros((rows, 1), F32)
    for h in range(HEADS):
        bias = jnp.where(row_h == h, bsb_ref[h], bias)
    head_of_lane = lax.broadcasted_iota(I32, (HEADS, SB_W), 1) // HEAD_DIM
    head_mask = head_of_lane == lax.broadcasted_iota(I32, (HEADS, SB_W), 0)

    @pl.when(ci == 0)
    def _init():
        q = q_ref[0] * (HEAD_DIM ** -0.5)
        qbd = jnp.concatenate(
            [jnp.where(head_mask, jnp.broadcast_to(q[t:t + 1, :], (HEADS, SB_W)), 0.0) for t in range(nt)],
            axis=0)
        qbd_ref[...] = jnp.concatenate(_split_bf16(qbd), axis=0)
        kown = kown_ref[0]
        vown = vown_ref[0]
        c = jnp.zeros((rows, 1), F32)
        acc = jnp.zeros((rows, SB_W), F32)
        for s in range(nt - 1, -1, -1):
            z = jnp.sum(qbd * kown[s:s + 1, :], axis=-1, keepdims=True) + bias
            vis = s < row_t
            l1, lb = _log_beta_terms(z, accurate=True)
            l1 = jnp.where(vis, l1, 0.0)
            w = jnp.where(vis, jnp.exp(lb + c), 0.0)
            acc = acc + w * vown[s:s + 1, :]
            c = c + l1
        acc_ref[...] = acc
        c_ref[...] = jnp.broadcast_to(c, (rows, LANES))

    later_mat = (lax.broadcasted_iota(I32, (page, page), 0) > lax.broadcasted_iota(I32, (page, page), 1)).astype(BF16)
    nt_dims = (((1,), (1,)), ((), ()))
    q2 = qbd_ref[...]
    q_hi = q2[:rows]

    def fold(x):
        return x[:rows] + x[rows:]

    ss = []
    for i in range(npg):
        kt_hi, kt_lo = _split_bf16(kp[i][0])
        ss.append(fold(jnp.dot(q2, kt_hi, preferred_element_type=F32))
                  + jnp.dot(q_hi, kt_lo, preferred_element_type=F32) + bias)
    l1s, lbs, exs = [], [], []
    for i in range(npg):
        l1, lb = _log_beta_terms(ss[i], accurate=True)
        l1s.append(l1)
        lbs.append(lb)
        exs.append(fold(jnp.dot(jnp.concatenate(_split_bf16(l1), axis=0), later_mat,
                                preferred_element_type=F32)))
    c = c_ref[...]
    acc = acc_ref[...]
    for i in range(npg):
        w = jnp.exp(lbs[i] + exs[i] + c)
        w_hi, w_lo = _split_bf16(w)
        vt_hi, vt_lo = _split_bf16(vp[i][0])
        acc = (acc + fold(lax.dot_general(jnp.concatenate([w_hi, w_lo], axis=0), vt_hi, nt_dims,
                                          preferred_element_type=F32))
               + lax.dot_general(w_hi, vt_lo, nt_dims, preferred_element_type=F32))
        c = c + jnp.broadcast_to(exs[i][:, 0:1] + l1s[i][:, 0:1], (rows, LANES))
    acc_ref[...] = acc
    c_ref[...] = c

    @pl.when(ci == nc - 1)
    def _fin():
        ys = [jnp.sum(jnp.where(head_mask, acc[t * HEADS:(t + 1) * HEADS, :], 0.0), axis=0, keepdims=True)
              for t in range(nt)]
        o_ref[0] = _rms(jnp.concatenate(ys, axis=0), g_ref[...])


def _sb_sample(page_table, b_sb, q_s, k_own, v_own, g_sb, cache_kt, cache_vt):
    nseq, npages = page_table.shape
    nt = q_s.shape[1]
    page = cache_kt.shape[2]
    npg = PAGES_PER_STEP
    nc = npages // npg
    rows = nt * HEADS

    def page_spec(i):
        return pl.BlockSpec(
            (1, SB_W, page),
            lambda s, c, pt, i=i: (pt[s * npages + (npages - 1) - (c * npg + i)], 0, 0))

    seq_spec = pl.BlockSpec((1, nt, SB_W), lambda s, c, pt: (s, 0, 0))
    grid_spec = pltpu.PrefetchScalarGridSpec(
        num_scalar_prefetch=1,
        grid=(nseq, nc),
        in_specs=[pl.BlockSpec(memory_space=pltpu.SMEM), seq_spec, seq_spec, seq_spec,
                  pl.BlockSpec((1, SB_W), lambda s, c, pt: (0, 0))]
                 + [page_spec(i) for i in range(npg)] + [page_spec(i) for i in range(npg)],
        out_specs=pl.BlockSpec((1, nt, SB_W), lambda s, c, pt: (s, 0, 0)),
        scratch_shapes=[pltpu.VMEM((2 * rows, SB_W), BF16), pltpu.VMEM((rows, SB_W), F32),
                        pltpu.VMEM((rows, LANES), F32)],
    )
    return pl.pallas_call(
        functools.partial(_sb_sample_kernel, npg=npg, nt=nt, page=page),
        grid_spec=grid_spec,
        out_shape=jax.ShapeDtypeStruct((nseq, nt, SB_W), F32),
        compiler_params=pltpu.CompilerParams(
            dimension_semantics=("arbitrary", "arbitrary"), vmem_limit_bytes=VMEM_LIMIT),
        name="sb_sample",
    )(page_table.reshape(-1), b_sb, q_s, k_own, v_own, g_sb, *([cache_kt] * npg), *([cache_vt] * npg))


def _outproj_router_kernel(x_ref, yl_ref, ys_ref, wout_ref, gffn_ref, wr_ref, br_ref,
                           x1_ref, h8_ref, e_ref, w_ref, *, tm, precise):
    lru_rows = slice(0, LRU_W)
    sb_rows = slice(LRU_W, LRU_W + SB_W)
    if precise:
        x1 = (x_ref[...] + _dot3(yl_ref[...], wout_ref[lru_rows, :]) + _dot3(ys_ref[...], wout_ref[sb_rows, :]))
    else:
        x1 = (x_ref[...]
              + jnp.dot(yl_ref[...], wout_ref[lru_rows, :], preferred_element_type=F32)
              + jnp.dot(ys_ref[...], wout_ref[sb_rows, :], preferred_element_type=F32))
    x1_ref[...] = x1
    h = _rms(x1, gffn_ref[...])
    for j in range(ROW_CHUNKS):
        h8_ref[pl.ds(j, tm, stride=ROW_CHUNKS), :] = h[:, j * LANES:(j + 1) * LANES]

    logits = _dot3(h, wr_ref[...]) + br_ref[...]

    lane = lax.broadcasted_iota(I32, (tm, ROUTER_COLS), 1)
    big = jnp.asarray(ROUTER_COLS, I32)
    neg = jnp.asarray(-jnp.inf, F32)
    gl = jnp.where(lane < N_GROUPS, logits, neg)
    gmax = jnp.max(gl, axis=-1, keepdims=True)
    g_idx = jnp.min(jnp.where(gl == gmax, lane, big), axis=-1, keepdims=True)
    p_g = 1.0 / jnp.sum(jnp.exp(gl - gmax), axis=-1, keepdims=True)
    e_lo = N_GROUPS + g_idx * EXPERTS_PER_GROUP
    el = jnp.where((lane >= e_lo) & (lane < e_lo + EXPERTS_PER_GROUP), logits, neg)
    m1 = jnp.max(el, axis=-1, keepdims=True)
    i1 = jnp.min(jnp.where(el == m1, lane, big), axis=-1, keepdims=True)
    el2 = jnp.where(lane == i1, neg, el)
    m2 = jnp.max(el2, axis=-1, keepdims=True)
    i2 = jnp.min(jnp.where(el2 == m2, lane, big), axis=-1, keepdims=True)
    r = jnp.exp(m2 - m1)
    w1 = p_g / (1.0 + r)
    w2 = w1 * r
    e_ref[...] = jnp.where(lane == 0, i1 - N_GROUPS, jnp.where(lane == 1, i2 - N_GROUPS, 0))
    w_ref[...] = jnp.where(lane == 0, w1, jnp.where(lane == 1, w2, 0.0))


def _outproj_router(x2, ylru, ysb, w_out, g_ffn, w_r, b_r, tm):
    n = x2.shape[0]
    precise = ylru.dtype == F32
    assert ysb.dtype == ylru.dtype == w_out.dtype
    row_spec = lambda w: pl.BlockSpec((tm, w), lambda t: (t, 0))
    full = lambda a: pl.BlockSpec(a.shape, lambda t: (0,) * a.ndim)
    return pl.pallas_call(
        functools.partial(_outproj_router_kernel, tm=tm, precise=precise),
        grid=(n // tm,),
        in_specs=[row_spec(D_MODEL), row_spec(LRU_W), row_spec(SB_W), full(w_out), full(g_ffn), full(w_r),
                  full(b_r)],
        out_specs=(row_spec(D_MODEL), pl.BlockSpec((tm * ROW_CHUNKS, LANES), lambda t: (t, 0)),
                   row_spec(ROUTER_COLS), row_spec(ROUTER_COLS)),
        out_shape=(jax.ShapeDtypeStruct((n, D_MODEL), F32),
                   jax.ShapeDtypeStruct((n * ROW_CHUNKS, LANES), F32),
                   jax.ShapeDtypeStruct((n, ROUTER_COLS), I32),
                   jax.ShapeDtypeStruct((n, ROUTER_COLS), F32)),
        compiler_params=pltpu.CompilerParams(
            dimension_semantics=("arbitrary",), vmem_limit_bytes=VMEM_LIMIT),
        name="outproj_router",
    )(x2, ylru, ysb, w_out, g_ffn, w_r, b_r)


def _row_tile(ref, row):
    return ref.at[pl.ds(pl.multiple_of(row * ROW_CHUNKS, ROW_CHUNKS), ROW_CHUNKS)]


def _for_each_row(td, fn):
    def body(r8, c):
        for k in range(SUBLANES):
            fn(r8 * SUBLANES + k)
        return c
    lax.fori_loop(0, td // SUBLANES, body, 0)


def _dispatch_kernel(pos_ref, h8_ref, xs_in_hbm, xs_hbm, stage_ref, sem, *, td):
    del xs_in_hbm
    t = pl.program_id(0)
    nt = pl.num_programs(0)
    slot = lax.rem(t, 2)

    def wait_slot(s):
        for _ in range(2):
            pltpu.make_async_copy(stage_ref.at[s], xs_hbm.at[pl.ds(0, td * ROW_CHUNKS)], sem.at[s]).wait()

    @pl.when(t >= 2)
    def _slot_free():
        wait_slot(slot)

    stage_ref[slot] = h8_ref[...]

    def start(r):
        for k in range(2):
            pltpu.make_async_copy(_row_tile(stage_ref.at[slot], r), _row_tile(xs_hbm, pos_ref[0, 0, 2 * r + k]),
                                  sem.at[slot]).start()

    _for_each_row(td, start)

    @pl.when(t == nt - 1)
    def _drain():
        @pl.when(t >= 1)
        def _():
            wait_slot(1 - slot)
        wait_slot(slot)


def _dispatch(pos, h8, xs8, td):
    n = pos.shape[0]
    pos3 = pos.reshape(n // td, 1, 2 * td)
    return pl.pallas_call(
        functools.partial(_dispatch_kernel, td=td),
        grid=(n // td,),
        in_specs=[pl.BlockSpec((1, 1, 2 * td), lambda t: (t, 0, 0), memory_space=pltpu.SMEM),
                  pl.BlockSpec((td * ROW_CHUNKS, LANES), lambda t: (t, 0)),
                  pl.BlockSpec(memory_space=pl.ANY)],
        out_specs=pl.BlockSpec(memory_space=pl.ANY),
        out_shape=jax.ShapeDtypeStruct(xs8.shape, xs8.dtype),
        scratch_shapes=[pltpu.VMEM((2, td * ROW_CHUNKS, LANES), F32), pltpu.SemaphoreType.DMA((2,))],
        input_output_aliases={2: 0},
        compiler_params=pltpu.CompilerParams(
            dimension_semantics=("arbitrary",), vmem_limit_bytes=VMEM_LIMIT),
        name="dispatch",
    )(pos3, h8, xs8)


def _experts_kernel(te_ref, tv_ref, tf_ref, xs_ref, wg_ref, wu_ref, wd_ref, ys_ref,
                    x2_ref, wgb_ref, wub_ref, wdb_ref, *, tm):
    del te_ref
    t = pl.program_id(0)

    @pl.when(tf_ref[t] == 1)
    def _cast_weights():
        wgb_ref[...] = wg_ref[0].astype(BF16)
        wub_ref[...] = wu_ref[0].astype(BF16)
        wdb_ref[...] = wd_ref[0].astype(BF16)

    @pl.when(tv_ref[t] == 1)
    def _compute():
        for j in range(ROW_CHUNKS):
            x2_ref[:, j * LANES:(j + 1) * LANES] = xs_ref[pl.ds(j, tm, stride=ROW_CHUNKS), :].astype(BF16)
        x = x2_ref[...]
        g = jnp.dot(x, wgb_ref[...], preferred_element_type=F32)
        u = jnp.dot(x, wub_ref[...], preferred_element_type=F32)
        act = (g * jax.nn.sigmoid(g) * u).astype(BF16)
        y = jnp.dot(act, wdb_ref[...], preferred_element_type=F32)
        for j in range(ROW_CHUNKS):
            ys_ref[pl.ds(j, tm, stride=ROW_CHUNKS), :] = y[:, j * LANES:(j + 1) * LANES]

    @pl.when(tv_ref[t] == 0)
    def _pad():
        ys_ref[...] = jnp.zeros_like(ys_ref)


def _experts(tile_e, tile_valid, tile_first, xs8, w_eg, w_eu, w_ed, tm):
    n_tiles = tile_e.shape[0]
    grid_spec = pltpu.PrefetchScalarGridSpec(
        num_scalar_prefetch=3,
        grid=(n_tiles,),
        in_specs=[
            pl.BlockSpec((tm * ROW_CHUNKS, LANES), lambda t, te, tv, tf: (t, 0)),
            pl.BlockSpec((1, D_MODEL, D_EXPERT), lambda t, te, tv, tf: (te[t], 0, 0)),
            pl.BlockSpec((1, D_MODEL, D_EXPERT), lambda t, te, tv, tf: (te[t], 0, 0)),
            pl.BlockSpec((1, D_EXPERT, D_MODEL), lambda t, te, tv, tf: (te[t], 0, 0)),
        ],
        out_specs=pl.BlockSpec((tm * ROW_CHUNKS, LANES), lambda t, te, tv, tf: (t, 0)),
        scratch_shapes=[pltpu.VMEM((tm, D_MODEL), BF16), pltpu.VMEM((D_MODEL, D_EXPERT), BF16),
                        pltpu.VMEM((D_MODEL, D_EXPERT), BF16), pltpu.VMEM((D_EXPERT, D_MODEL), BF16)],
    )
    return pl.pallas_call(
        functools.partial(_experts_kernel, tm=tm),
        grid_spec=grid_spec,
        out_shape=jax.ShapeDtypeStruct(xs8.shape, F32),
        compiler_params=pltpu.CompilerParams(
            dimension_semantics=("arbitrary",), vmem_limit_bytes=VMEM_LIMIT),
        name="experts",
    )(tile_e, tile_valid, tile_first, xs8, w_eg, w_eu, w_ed)


def _combine_kernel(pos_ref, posn_ref, x1_ref, wsel_ref, gfin_ref, ys_hbm, o_ref, buf_ref, sem, *, td):
    t = pl.program_id(0)
    nt = pl.num_programs(0)
    slot = lax.rem(t, 2)

    def start_gather(idx_ref, s):
        def start(r):
            for k in range(2):
                pltpu.make_async_copy(_row_tile(ys_hbm, idx_ref[0, 0, 2 * r + k]), _row_tile(buf_ref.at[s, k], r),
                                      sem.at[s]).start()
        _for_each_row(td, start)

    @pl.when(t == 0)
    def _first():
        start_gather(pos_ref, slot)

    @pl.when(t + 1 < nt)
    def _next():
        start_gather(posn_ref, 1 - slot)

    for k in range(2):
        pltpu.make_async_copy(ys_hbm.at[pl.ds(0, td * ROW_CHUNKS)], buf_ref.at[slot, k], sem.at[slot]).wait()

    w = wsel_ref[...]
    w0 = w[:, 0:1]
    w1 = w[:, 1:2]
    cols = []
    for j in range(ROW_CHUNKS):
        cols.append(x1_ref[:, j * LANES:(j + 1) * LANES]
                    + w0 * buf_ref[slot, 0, pl.ds(j, td, stride=ROW_CHUNKS), :]
                    + w1 * buf_ref[slot, 1, pl.ds(j, td, stride=ROW_CHUNKS), :])
    o_ref[...] = _rms(jnp.concatenate(cols, axis=1), gfin_ref[...])


def _combine_final(pos, x1, wsel, g_final, ys8, td):
    n = pos.shape[0]
    nt = n // td
    pos3 = pos.reshape(nt, 1, 2 * td)
    idx_spec = lambda fn: pl.BlockSpec((1, 1, 2 * td), fn, memory_space=pltpu.SMEM)
    return pl.pallas_call(
        functools.partial(_combine_kernel, td=td),
        grid=(nt,),
        in_specs=[idx_spec(lambda t: (t, 0, 0)), idx_spec(lambda t: (jnp.minimum(t + 1, nt - 1), 0, 0)),
                  pl.BlockSpec((td, D_MODEL), lambda t: (t, 0)),
                  pl.BlockSpec((td, ROUTER_COLS), lambda t: (t, 0)),
                  pl.BlockSpec((1, D_MODEL), lambda t: (0, 0)),
                  pl.BlockSpec(memory_space=pl.ANY)],
        out_specs=pl.BlockSpec((td, D_MODEL), lambda t: (t, 0)),
        out_shape=jax.ShapeDtypeStruct((n, D_MODEL), F32),
        scratch_shapes=[pltpu.VMEM((2, 2, td * ROW_CHUNKS, LANES), F32), pltpu.SemaphoreType.DMA((2,))],
        compiler_params=pltpu.CompilerParams(
            dimension_semantics=("arbitrary",), vmem_limit_bytes=VMEM_LIMIT),
        name="combine_final",
    )(pos3, pos3, x1, wsel, g_final, ys8)


def _routing_plan(e_sel, tm, n_tiles):
    e_flat = e_sel.reshape(-1)
    onehot = (e_flat[:, None] == jnp.arange(N_EXPERTS, dtype=I32)[None, :]).astype(I32)
    csum = jnp.cumsum(onehot, axis=0)
    rank = jnp.sum(onehot * csum, axis=1) - 1
    counts = csum[-1]
    tiles_per = (counts + tm - 1) // tm
    tile_end = jnp.cumsum(tiles_per)
    tile_start = tile_end - tiles_per
    pos = (tile_start * tm)[e_flat] + rank
    total = tile_end[-1]
    t = jnp.arange(n_tiles, dtype=I32)
    tile_e = jnp.minimum(jnp.sum((t[:, None] >= tile_end[None, :]).astype(I32), axis=1), N_EXPERTS - 1)
    last_e = jnp.max(jnp.where(counts > 0, jnp.arange(N_EXPERTS, dtype=I32), 0))
    valid = t < total
    tile_e = jnp.where(valid, tile_e, last_e).astype(I32)
    first = (valid & (t == tile_start[tile_e])).astype(I32)
    return pos.reshape(-1, 2).astype(I32), tile_e, valid.astype(I32), first


def _block_diag(w):
    nb, d, _ = w.shape
    eye = jnp.eye(nb, dtype=w.dtype)
    return (eye[:, None, :, None] * w[:, :, None, :]).reshape(nb * d, nb * d)


def kernel(x_prompt, x_sample, cache_k, cache_v, state_lru_h, state_conv, page_table, g_mix, w_in, conv_w, conv_b, w_a, b_a, w_i, b_i, lam, b_sb, g_out_lru, g_out_sb, w_out, g_ffn, w_rg, b_rg, w_re, b_re, w_eg, w_eu, w_ed, g_final):
    depth = w_in.shape[0]
    assert depth == 1, "single-layer trunk"
    batch, seq, _ = x_prompt.shape
    nseq, nt, _ = x_sample.shape
    npool, page = cache_k.shape[1], cache_k.shape[2]
    n_p = batch * seq
    n_s = nseq * nt
    row = lambda a: a.reshape(1, -1)

    w_gates = jnp.concatenate([_block_diag(w_a[0]), _block_diag(w_i[0])], axis=1)
    b_gates = jnp.concatenate([b_a[0], b_i[0]]).reshape(1, -1)
    w_r = jnp.zeros((D_MODEL, ROUTER_COLS), F32)
    w_r = w_r.at[:, :N_GROUPS].set(w_rg[0]).at[:, N_GROUPS:N_GROUPS + N_EXPERTS].set(w_re[0])
    b_r = jnp.zeros((1, ROUTER_COLS), F32)
    b_r = b_r.at[0, :N_GROUPS].set(b_rg[0]).at[0, N_GROUPS:N_GROUPS + N_EXPERTS].set(b_re[0])
    lru_tail = (row(lam[0]), row(g_out_lru[0]))

    x_p = x_prompt.reshape(n_p, D_MODEL)
    q_p, kt_p, vt_p, kb_p, vtb_p, ylru_p, ht_p, cbuf_p = _inproj_lru_prompt(
        x_p, row(g_mix[0]), w_in[0].astype(BF16), conv_w[0], row(conv_b[0]), w_gates.astype(BF16), b_gates,
        *lru_tail, batch, seq)
    ysb_p = _sb_prompt(b_sb[0], q_p, kb_p, vtb_p, row(g_out_sb[0]), batch, seq)

    x_s = x_sample.transpose(1, 0, 2).reshape(n_s, D_MODEL)
    buf_tm = state_conv[0].transpose(1, 0, 2).reshape((CONV_W - 1) * nseq, LRU_W)
    q_s, k_s, v_s, ylru_s, ht_s, cbuf_s = _inproj_lru_sample(
        x_s, state_lru_h[0], buf_tm, row(g_mix[0]), w_in[0], conv_w[0], row(conv_b[0]), w_gates, b_gates,
        *lru_tail, nseq, nt)
    seq_major = lambda a, w: a.reshape(-1, nseq, w).transpose(1, 0, 2)
    q_sm, k_sm, v_sm = seq_major(q_s, SB_W), seq_major(k_s, SB_W), seq_major(v_s, SB_W)
    pages_t = lambda c: c[0].transpose(0, 2, 3, 1).reshape(npool, SB_W, page)
    ysb_s = _sb_sample(page_table, b_sb[0], q_sm, k_sm, v_sm, row(g_out_sb[0]),
                       pages_t(cache_k), pages_t(cache_v))
    x_s2 = x_sample.reshape(n_s, D_MODEL)
    ylru_s2 = seq_major(ylru_s, LRU_W).reshape(n_s, LRU_W)
    ysb_s2 = ysb_s.reshape(n_s, SB_W)

    router_args = (row(g_ffn[0]), w_r, b_r)
    x1_p, h8_p, e_p, wsel_p = _outproj_router(x_p, ylru_p, ysb_p, w_out[0].astype(BF16), *router_args, TM_OUT)
    x1_s, h8_s, e_s, wsel_s = _outproj_router(x_s2, ylru_s2, ysb_s2, w_out[0], *router_args, n_s)

    n_all = n_p + n_s
    n_tiles = -(-2 * n_all // TM_EXP) + N_EXPERTS
    e_all = jnp.concatenate([e_p[:, :2], e_s[:, :2]], axis=0)
    pos, tile_e, tile_valid, tile_first = _routing_plan(e_all, TM_EXP, n_tiles)
    pos_p, pos_s = pos[:n_p], pos[n_p:]

    xs8 = jnp.zeros((n_tiles * TM_EXP * ROW_CHUNKS, LANES), F32)
    xs8 = _dispatch(pos_p, h8_p, xs8, TD)
    xs8 = _dispatch(pos_s, h8_s, xs8, n_s)
    ys8 = _experts(tile_e, tile_valid, tile_first, xs8, w_eg[0], w_eu[0], w_ed[0], TM_EXP)

    gfin = row(g_final)
    y_p = _combine_final(pos_p, x1_p, wsel_p, gfin, ys8, TD)
    y_s = _combine_final(pos_s, x1_s, wsel_s, gfin, ys8, n_s)

    heads = lambda a, lead: a.reshape(*lead, HEADS, HEAD_DIM)
    from_t = lambda a: a.reshape(batch, HEADS, HEAD_DIM, seq).transpose(0, 3, 1, 2)[None]
    return (
        y_p.reshape(batch, seq, D_MODEL),
        y_s.reshape(nseq, nt, D_MODEL),
        from_t(kt_p),
        from_t(vt_p),
        ht_p.reshape(1, batch, LRU_W),
        cbuf_p.reshape(1, batch, CONV_W - 1, LRU_W),
        heads(k_sm, (1, nseq, nt)),
        heads(v_sm, (1, nseq, nt)),
        ht_s.reshape(1, nseq, LRU_W),
        cbuf_s.reshape(CONV_W - 1, nseq, LRU_W).transpose(1, 0, 2)[None],
    )
```
